```python
import jax, jax.numpy as jnp
from jax import lax
import numpy as np

D_MODEL = 1024
BATCH = 8
SEQ = 4096
DEPTH = 2
DEC_BATCH = 128
DEC_SEQ = 1
PAST_LEN = 16384
PAGE_SIZE = 128

N_EVEN_LAYERS = (DEPTH + 1) // 2
N_ODD_LAYERS = DEPTH // 2
POOL_WIDTH = D_MODEL // 2
POOL_WINDOWS = (2, 4, 8, 16)
N_POOL_GROUPS = len(POOL_WINDOWS)
POOL_GROUP = POOL_WIDTH // N_POOL_GROUPS
POOL_BUF = max(POOL_WINDOWS) - 1
MLA_HEADS = 8
QK_NOPE = 64
QK_ROPE = 32
V_HEAD = 64
Q_LORA = D_MODEL // 4
KV_LORA = D_MODEL // 8
MLA_ROW = KV_LORA + QK_ROPE
MLA_SCALE = (QK_NOPE + QK_ROPE) ** -0.5
ROPE_THETA = 10000.0
Q_BLOCK = 128
D_IN0 = POOL_WIDTH + Q_LORA + KV_LORA + QK_ROPE
D_OUT0 = POOL_WIDTH + MLA_HEADS * V_HEAD
D_RNN = D_MODEL
LRU_HEADS = 8
LRU_BLOCK = D_RNN // LRU_HEADS
CONV_WIDTH = 4
LRU_C = 8.0
D_FF = 11 * D_MODEL // 4
N_EXPERTS = 8
TOP_K = 2
D_EXPERT = D_FF // 2
PLE_DIM = 256
NORM_EPS = 1e-6

kernel_name = "hybrid_pool_mla_rglru_moe_step"


def _rmsnorm(x, g):
    xf = x.astype(jnp.float32)
    y = xf * lax.rsqrt(jnp.mean(xf * xf, axis=-1, keepdims=True) + NORM_EPS)
    return y.astype(x.dtype) * g


def _rope_tables(pos):
    inv = ROPE_THETA ** (-jnp.arange(0, QK_ROPE, 2, dtype=jnp.float32) / QK_ROPE)
    ang = pos.astype(jnp.float32)[:, None] * inv[None, :]
    return jnp.cos(ang), jnp.sin(ang)


def _apply_rope(x, cos, sin):
    half = QK_ROPE // 2
    xf = x.astype(jnp.float32)
    x1, x2 = xf[..., :half], xf[..., half:]
    out = jnp.concatenate([x1 * cos - x2 * sin, x2 * cos + x1 * sin], axis=-1)
    return out.astype(x.dtype)


def _pool_mixer(u, buf, pos, w_pool, scale):
    bsz, t, _ = u.shape
    ext = jnp.concatenate([buf, u], axis=1)
    cs = lax.cumsum(ext.astype(jnp.float32), axis=1)
    cs = jnp.concatenate([jnp.zeros((bsz, 1, POOL_WIDTH), jnp.float32), cs], axis=1)
    end = cs[:, POOL_BUF + 1:POOL_BUF + 1 + t]
    means = []
    for g, w in enumerate(POOL_WINDOWS):
        sl = slice(g * POOL_GROUP, (g + 1) * POOL_GROUP)
        start = cs[:, POOL_BUF + 1 - w:POOL_BUF + 1 - w + t, sl]
        cnt = jnp.minimum(pos + 1, w).astype(jnp.float32)[None, :, None]
        means.append((end[..., sl] - start) / cnt)
    pooled = jnp.concatenate(means, axis=-1).astype(u.dtype) - u
    mixed = jnp.einsum("btgc,gcd->btgd", pooled.reshape(bsz, t, N_POOL_GROUPS, POOL_GROUP), w_pool)
    y = mixed.reshape(bsz, t, POOL_WIDTH) * scale
    return y, ext[:, -POOL_BUF:]


def _causal_block_attention(q, kv):
    bsz, t, h, c = q.shape
    nb = t // Q_BLOCK
    qb = q.reshape(bsz, nb, Q_BLOCK, h, c).transpose(1, 0, 2, 3, 4)
    k_pos = jnp.arange(t)
    lat = kv[..., :KV_LORA]

    def one_block(args):
        q_blk, blk = args
        s = jnp.einsum("bqhc,bsc->bhqs", q_blk, kv).astype(jnp.float32)
        q_pos = blk * Q_BLOCK + jnp.arange(Q_BLOCK)
        s = jnp.where(k_pos[None, :] <= q_pos[:, None], s, -jnp.inf)
        p = jax.nn.softmax(s, axis=-1).astype(kv.dtype)
        return jnp.einsum("bhqs,bsc->bqhc", p, lat)

    o = lax.map(one_block, (qb, jnp.arange(nb)))
    return o.transpose(1, 0, 2, 3, 4).reshape(bsz, t, h, KV_LORA)


def _decode_attention(q, past, rows):
    t = q.shape[1]
    n_past = past.shape[1]
    s_past = jnp.einsum("bthc,bsc->bhts", q, past).astype(jnp.float32)
    s_new = jnp.einsum("bthc,bsc->bhts", q, rows).astype(jnp.float32)
    causal = jnp.tril(jnp.ones((t, t), dtype=bool))
    s_new = jnp.where(causal, s_new, -jnp.inf)
    p = jax.nn.softmax(jnp.concatenate([s_past, s_new], axis=-1), axis=-1).astype(q.dtype)
    return (jnp.einsum("bhts,bsc->bthc", p[..., :n_past], past[..., :KV_LORA])
            + jnp.einsum("bhts,bsc->bthc", p[..., n_past:], rows[..., :KV_LORA]))


def _pool_mla_mixer(xn, pos, pool_buf, past, prm, j):
    bsz, t, _ = xn.shape
    proj = xn @ prm["w_in0"][j]
    o1 = POOL_WIDTH
    o2 = o1 + Q_LORA
    o3 = o2 + KV_LORA
    u, c_q, c_kv, k_r = proj[..., :o1], proj[..., o1:o2], proj[..., o2:o3], proj[..., o3:]
    pool_y, new_pool = _pool_mixer(u, pool_buf, pos, prm["pool_w"][j], prm["pool_scale"][j])
    q = (_rmsnorm(c_q, prm["q_norm"][j]) @ prm["w_uq"][j]).reshape(bsz, t, MLA_HEADS, QK_NOPE + QK_ROPE)
    q_nope, q_rope = q[..., :QK_NOPE], q[..., QK_NOPE:]
    cos, sin = _rope_tables(pos)
    q_rope = _apply_rope(q_rope, cos[None, :, None, :], sin[None, :, None, :])
    k_rope = _apply_rope(k_r, cos[None], sin[None])
    rows = jnp.concatenate([_rmsnorm(c_kv, prm["kv_norm"][j]), k_rope], axis=-1)
    q_lat = jnp.einsum("bthn,chn->bthc", q_nope, prm["w_uk"][j])
    q_cat = jnp.concatenate([q_lat, q_rope], axis=-1) * MLA_SCALE
    if past is None:
        o_lat = _causal_block_attention(q_cat, rows)
    else:
        o_lat = _decode_attention(q_cat, past, rows)
    o = jnp.einsum("bthc,chv->bthv", o_lat, prm["w_uv"][j]).reshape(bsz, t, MLA_HEADS * V_HEAD)
    y = jnp.concatenate([pool_y, o], axis=-1) @ prm["w_out0"][j]
    return y, new_pool, rows


def _linear_scan(a, b, h0):
    b = b.at[:, 0].add(a[:, 0] * h0)

    def combine(lhs, rhs):
        a_l, b_l = lhs
        a_r, b_r = rhs
        return a_l * a_r, a_r * b_l + b_r

    _, h = lax.associative_scan(combine, (a, b), axis=1)
    return h


def _rglru_mixer(xn, conv_buf, h0, prm, j):
    bsz, t, _ = xn.shape
    proj = xn @ prm["w_in1"][j]
    xb, gb = proj[..., :D_RNN], proj[..., D_RNN:]
    ext = jnp.concatenate([conv_buf, xb], axis=1)
    w = prm["conv_w"][j]
    xc = prm["conv_b"][j] + ext[:, 0:t] * w[0]
    for k in range(1, CONV_WIDTH):
        xc = xc + ext[:, k:k + t] * w[k]
    xh = xc.reshape(bsz, t, LRU_HEADS, LRU_BLOCK)
    r = jax.nn.sigmoid(jnp.einsum("bthi,hij->bthj", xh, prm["w_rg"][j]).reshape(bsz, t, D_RNN) + prm["b_rg"][j])
    ig = jax.nn.sigmoid(jnp.einsum("bthi,hij->bthj", xh, prm["w_ig"][j]).reshape(bsz, t, D_RNN) + prm["b_ig"][j])
    log_a = (-LRU_C * r.astype(jnp.float32)) * jax.nn.softplus(-prm["lru_lambda"][j].astype(jnp.float32))
    a = jnp.exp(log_a)
    mult = jnp.sqrt(-jnp.expm1(2.0 * log_a))
    bt = mult * (ig * xc).astype(jnp.float32)
    h = _linear_scan(a, bt, h0.astype(jnp.float32))
    y = h.astype(xn.dtype) * jax.nn.gelu(gb)
    return y @ prm["w_out1"][j], ext[:, -(CONV_WIDTH - 1):], h[:, -1].astype(xn.dtype)


def _swiglu(h, w_g, w_u, w_d):
    return (jax.nn.silu(h @ w_g) * (h @ w_u)) @ w_d


def _moe_swiglu(h, w_router, w_g, w_u, w_d):
    logits = (h @ w_router).astype(jnp.float32)
    top_v, top_i = lax.top_k(logits, TOP_K)
    top_w = jax.nn.softmax(top_v, axis=-1)
    combine = jnp.einsum("btk,btke->bte", top_w, jax.nn.one_hot(top_i, N_EXPERTS, dtype=jnp.float32)).astype(h.dtype)
    out = jnp.zeros_like(h)
    for e in range(N_EXPERTS):
        out = out + combine[..., e:e + 1] * _swiglu(h, w_g[e], w_u[e], w_d[e])
    return out


def _per_layer_embedding(x, p_l, g_norm, w_proj, w_gate):
    gate = jax.nn.sigmoid(_rmsnorm(x, g_norm) @ w_gate)
    return gate * (p_l @ w_proj)


def _run_group(x, p, pos, pool_buf, conv_buf, lru_h, cache_mla, page_table, prm):
    rows_out, pool_out, conv_out, h_out = [], [], [], []
    for layer in range(DEPTH):
        j = layer // 2
        xn = _rmsnorm(x, prm["norm_mix"][layer])
        if layer % 2 == 0:
            past = None
            if cache_mla is not None:
                past = cache_mla[j, page_table].reshape(x.shape[0], -1, MLA_ROW)
            mix, new_pool, rows = _pool_mla_mixer(xn, pos, pool_buf[j], past, prm, j)
            rows_out.append(rows)
            pool_out.append(new_pool)
            x = x + mix
            x = x + _swiglu(_rmsnorm(x, prm["norm_ffn"][layer]), prm["w_ffn_gate"][j],
                            prm["w_ffn_up"][j], prm["w_ffn_down"][j])
        else:
            mix, new_conv, new_h = _rglru_mixer(xn, conv_buf[j], lru_h[j], prm, j)
            conv_out.append(new_conv)
            h_out.append(new_h)
            x = x + mix
            x = x + _moe_swiglu(_rmsnorm(x, prm["norm_ffn"][layer]), prm["w_router"][j],
                                prm["w_exp_gate"][j], prm["w_exp_up"][j], prm["w_exp_down"][j])
        x = x + _per_layer_embedding(x, p[layer], prm["norm_ple"][layer],
                                     prm["w_ple_proj"][layer], prm["w_ple_gate"][layer])
    y = _rmsnorm(x, prm["norm_final"])
    return y, jnp.stack(rows_out), jnp.stack(pool_out), jnp.stack(conv_out), jnp.stack(h_out)


def setup_inputs(seed: int = 0) -> dict:
    key = jax.random.key(seed)
    ks = iter(jax.random.split(key, 64))
    f32 = jnp.float32
    n_pages = PAST_LEN // PAGE_SIZE
    n_pool = (DEC_BATCH * n_pages * 5) // 4
    ne, no = N_EVEN_LAYERS, N_ODD_LAYERS

    def nrm(shape, scale):
        return jax.random.normal(next(ks), shape, f32) * scale

    def gain(shape):
        return 1.0 + nrm(shape, 0.05)

    inp = {}
    inp["x_prompt"] = nrm((BATCH, SEQ, D_MODEL), 1.0)
    inp["x_sample"] = nrm((DEC_BATCH, DEC_SEQ, D_MODEL), 1.0)
    inp["p_prompt"] = nrm((DEPTH, BATCH, SEQ, PLE_DIM), 1.0)
    inp["p_sample"] = nrm((DEPTH, DEC_BATCH, DEC_SEQ, PLE_DIM), 1.0)
    inp["cache_mla"] = nrm((ne, n_pool, PAGE_SIZE, MLA_ROW), 1.0)
    inp["state_pool"] = nrm((ne, DEC_BATCH, POOL_BUF, POOL_WIDTH), 1.0)
    inp["state_conv"] = nrm((no, DEC_BATCH, CONV_WIDTH - 1, D_RNN), 1.0)
    inp["state_lru"] = nrm((no, DEC_BATCH, D_RNN), 0.5)
    perm = jax.random.permutation(next(ks), n_pool)[:DEC_BATCH * n_pages]
    inp["page_table"] = perm.astype(jnp.int32).reshape(DEC_BATCH, n_pages)
    inp["norm_mix"] = gain((DEPTH, D_MODEL))
    inp["norm_ffn"] = gain((DEPTH, D_MODEL))
    inp["norm_ple"] = gain((DEPTH, D_MODEL))
    inp["norm_final"] = gain((D_MODEL,))
    inp["w_in0"] = nrm((ne, D_MODEL, D_IN0), D_MODEL ** -0.5)
    inp["pool_w"] = nrm((ne, N_POOL_GROUPS, POOL_GROUP, POOL_GROUP), POOL_GROUP ** -0.5)
    inp["pool_scale"] = 1.0 + nrm((ne, POOL_WIDTH), 0.1)
    inp["q_norm"] = gain((ne, Q_LORA))
    inp["kv_norm"] = gain((ne, KV_LORA))
    inp["w_uq"] = nrm((ne, Q_LORA, MLA_HEADS * (QK_NOPE + QK_ROPE)), Q_LORA ** -0.5)
    inp["w_uk"] = nrm((ne, KV_LORA, MLA_HEADS, QK_NOPE), KV_LORA ** -0.5)
    inp["w_uv"] = nrm((ne, KV_LORA, MLA_HEADS, V_HEAD), KV_LORA ** -0.5)
    inp["w_out0"] = nrm((ne, D_OUT0, D_MODEL), D_OUT0 ** -0.5)
    inp["w_in1"] = nrm((no, D_MODEL, 2 * D_RNN), D_MODEL ** -0.5)
    inp["conv_w"] = nrm((no, CONV_WIDTH, D_RNN), CONV_WIDTH ** -0.5)
    inp["conv_b"] = nrm((no, D_RNN), 0.01)
    inp["w_rg"] = nrm((no, LRU_HEADS, LRU_BLOCK, LRU_BLOCK), LRU_BLOCK ** -0.5)
    inp["b_rg"] = nrm((no, D_RNN), 0.01)
    inp["w_ig"] = nrm((no, LRU_HEADS, LRU_BLOCK, LRU_BLOCK), LRU_BLOCK ** -0.5)
    inp["b_ig"] = nrm((no, D_RNN), 0.01)
    a0 = jax.random.uniform(next(ks), (no, D_RNN), f32, 0.9, 0.999)
    inp["lru_lambda"] = jnp.log(a0) - jnp.log1p(-a0)
    inp["w_out1"] = nrm((no, D_RNN, D_MODEL), D_RNN ** -0.5)
    inp["w_ffn_gate"] = nrm((ne, D_MODEL, D_FF), D_MODEL ** -0.5)
    inp["w_ffn_up"] = nrm((ne, D_MODEL, D_FF), D_MODEL ** -0.5)
    inp["w_ffn_down"] = nrm((ne, D_FF, D_MODEL), D_FF ** -0.5)
    inp["w_router"] = nrm((no, D_MODEL, N_EXPERTS), D_MODEL ** -0.5)
    inp["w_exp_gate"] = nrm((no, N_EXPERTS, D_MODEL, D_EXPERT), D_MODEL ** -0.5)
    inp["w_exp_up"] = nrm((no, N_EXPERTS, D_MODEL, D_EXPERT), D_MODEL ** -0.5)
    inp["w_exp_down"] = nrm((no, N_EXPERTS, D_EXPERT, D_MODEL), D_EXPERT ** -0.5)
    inp["w_ple_proj"] = nrm((DEPTH, PLE_DIM, D_MODEL), PLE_DIM ** -0.5)
    inp["w_ple_gate"] = nrm((DEPTH, D_MODEL, D_MODEL), D_MODEL ** -0.5)
    return inp


def reference(x_prompt, x_sample, p_prompt, p_sample, cache_mla, state_pool, state_conv, state_lru,
              page_table, norm_mix, norm_ffn, norm_ple, norm_final, w_in0, pool_w, pool_scale,
              q_norm, kv_norm, w_uq, w_uk, w_uv, w_out0, w_in1, conv_w, conv_b, w_rg, b_rg,
              w_ig, b_ig, lru_lambda, w_out1, w_ffn_gate, w_ffn_up, w_ffn_down, w_router,
              w_exp_gate, w_exp_up, w_exp_down, w_ple_proj, w_ple_gate):
    prm = dict(norm_mix=norm_mix, norm_ffn=norm_ffn, norm_ple=norm_ple, norm_final=norm_final,
               w_in0=w_in0, pool_w=pool_w, pool_scale=pool_scale, q_norm=q_norm, kv_norm=kv_norm,
               w_uq=w_uq, w_uk=w_uk, w_uv=w_uv, w_out0=w_out0, w_in1=w_in1, conv_w=conv_w,
               conv_b=conv_b, w_rg=w_rg, b_rg=b_rg, w_ig=w_ig, b_ig=b_ig, lru_lambda=lru_lambda,
               w_out1=w_out1, w_ffn_gate=w_ffn_gate, w_ffn_up=w_ffn_up, w_ffn_down=w_ffn_down,
               w_router=w_router, w_exp_gate=w_exp_gate, w_exp_up=w_exp_up,
               w_exp_down=w_exp_down, w_ple_proj=w_ple_proj, w_ple_gate=w_ple_gate)
    bsz, seq = x_prompt.shape[0], x_prompt.shape[1]
    dt = x_prompt.dtype
    past_len = page_table.shape[1] * PAGE_SIZE
    pos_prompt = jnp.arange(seq, dtype=jnp.int32)
    pos_sample = past_len + jnp.arange(x_sample.shape[1], dtype=jnp.int32)
    zero_pool = jnp.zeros((N_EVEN_LAYERS, bsz, POOL_BUF, POOL_WIDTH), dt)
    zero_conv = jnp.zeros((N_ODD_LAYERS, bsz, CONV_WIDTH - 1, D_RNN), dt)
    zero_h = jnp.zeros((N_ODD_LAYERS, bsz, D_RNN), dt)
    y_prompt, rows_p, pool_p, conv_p, h_p = _run_group(
        x_prompt, p_prompt, pos_prompt, zero_pool, zero_conv, zero_h, None, None, prm)
    y_sample, rows_s, pool_s, conv_s, h_s = _run_group(
        x_sample, p_sample, pos_sample, state_pool, state_conv, state_lru, cache_mla, page_table, prm)
    return (y_prompt, y_sample, rows_p, rows_s, pool_p, pool_s, conv_p, conv_s, h_p, h_s)
```

```python
import functools

import numpy as np
import jax
import jax.numpy as jnp
from jax import lax
from jax.experimental import pallas as pl
from jax.experimental.pallas import tpu as pltpu

NORM_EPS = 1e-6
ROPE_THETA = 10000.0
LRU_C = 8.0
POOL_WINDOWS = (2, 4, 8, 16)
POOL_BUF = max(POOL_WINDOWS) - 1
POOL_HALO = 16
CONV_WIDTH = 4
CONV_HALO = 8
MLA_HEADS = 8
QK_NOPE = 64
QK_ROPE = 32
V_HEAD = 64
KV_LORA = 128
MLA_ROW = KV_LORA + QK_ROPE
MLA_SCALE = (QK_NOPE + QK_ROPE) ** -0.5
TOP_K = 2
LANES = 128
SUBLANES = 8
DECODE_PAGES_PER_STEP = 32
VMEM_LIMIT = 56 * 1024 * 1024

_F32 = jnp.float32
_BF16 = jnp.bfloat16


def _bf(x):
    return x.astype(_BF16)


def _dot(a, b):
    return jnp.dot(a, b, preferred_element_type=_F32)


def _dot_nt(a, b):
    return lax.dot_general(a, b, (((1,), (1,)), ((), ())), preferred_element_type=_F32)


def _rms(x, g):
    ms = jnp.mean(x * x, axis=-1, keepdims=True)
    return (x * lax.rsqrt(ms + NORM_EPS)) * g


def _sigmoid(x):
    return 1.0 / (1.0 + jnp.exp(-x))


def _silu(x):
    return x * _sigmoid(x)


def _gelu_tanh(x):
    c = np.float32(np.sqrt(2.0 / np.pi))
    return 0.5 * x * (1.0 + jnp.tanh(c * (x + 0.044715 * (x * x * x))))


def _softplus(x):
    return jnp.maximum(x, 0.0) + jnp.log1p(jnp.exp(-jnp.abs(x)))


def _params(sem):
    return pltpu.CompilerParams(dimension_semantics=sem, vmem_limit_bytes=VMEM_LIMIT)


def _const_spec(shape):
    n = len(shape)
    return pl.BlockSpec(shape, lambda *_: (0,) * n)


def _swap_halves(v, first_half):
    return jnp.where(first_half, pltpu.roll(v, LANES - 16, 1), pltpu.roll(v, 16, 1))


def _l0_in_kernel(x_ref, g_ref, win_ref, qn_ref, wuq_ref, wuk_ref, kvn_ref, cos_ref, sin_ref,
                  u_ref, rows_ref, q_ref):
    tm = x_ref.shape[0]
    pool_w = u_ref.shape[1]
    xn = _rms(x_ref[...], g_ref[...])
    proj = _dot(_bf(xn), win_ref[...])
    u_ref[...] = proj[:, :pool_w]
    cos = cos_ref[...]
    sin = sin_ref[...]
    lane = lax.broadcasted_iota(jnp.int32, (tm, LANES), 1)
    first_half = (lane % QK_ROPE) < (QK_ROPE // 2)
    o_q = pool_w
    o_kv = o_q + wuq_ref.shape[0]
    o_kr = o_kv + KV_LORA
    rows_ref[:, :KV_LORA] = _rms(proj[:, o_kv:o_kr], kvn_ref[...])
    kr = proj[:, o_kr:o_kr + LANES]
    kr = kr * cos[:, :LANES] + _swap_halves(kr, first_half) * sin[:, :LANES]
    rows_ref[:, KV_LORA:MLA_ROW] = kr[:, :QK_ROPE]
    q = _dot(_bf(_rms(proj[:, o_q:o_kv], qn_ref[...])), wuq_ref[...])
    n_nope = MLA_HEADS * QK_NOPE
    heads_per_col = LANES // QK_ROPE
    for c in range(MLA_HEADS // heads_per_col):
        qr = q[:, n_nope + c * LANES:n_nope + (c + 1) * LANES]
        cs = cos[:, c * LANES:(c + 1) * LANES]
        sn = sin[:, c * LANES:(c + 1) * LANES]
        qrot = (qr * cs + _swap_halves(qr, first_half) * sn) * MLA_SCALE
        for k in range(heads_per_col):
            sh = qrot if k == 0 else pltpu.roll(qrot, LANES - QK_ROPE * k, 1)
            q_ref[c * heads_per_col + k, :, KV_LORA:MLA_ROW] = sh[:, :QK_ROPE]
    for j in range(MLA_HEADS // 2):
        lat2 = _dot(_bf(q[:, j * LANES:(j + 1) * LANES]), wuk_ref[j]) * MLA_SCALE
        q_ref[2 * j, :, :KV_LORA] = lat2[:, :KV_LORA]
        q_ref[2 * j + 1, :, :KV_LORA] = lat2[:, KV_LORA:]


def _l0_in(x, g, w_in, q_norm, w_uq, w_uk_bd, kv_norm, cos, sin, *, tm, pool_width):
    n, d = x.shape
    n_tab = cos.shape[0] // tm
    row = lambda i: (i, 0)
    tab = lambda i: (i % n_tab, 0)
    return pl.pallas_call(
        _l0_in_kernel,
        grid=(n // tm,),
        in_specs=[
            pl.BlockSpec((tm, d), row),
            _const_spec(g.shape), _const_spec(w_in.shape), _const_spec(q_norm.shape),
            _const_spec(w_uq.shape), _const_spec(w_uk_bd.shape), _const_spec(kv_norm.shape),
            pl.BlockSpec((tm, cos.shape[1]), tab), pl.BlockSpec((tm, sin.shape[1]), tab),
        ],
        out_specs=[
            pl.BlockSpec((tm, pool_width), row),
            pl.BlockSpec((tm, MLA_ROW), row),
            pl.BlockSpec((MLA_HEADS, tm, MLA_ROW), lambda i: (0, i, 0)),
        ],
        out_shape=[
            jax.ShapeDtypeStruct((n, pool_width), _F32),
            jax.ShapeDtypeStruct((n, MLA_ROW), _F32),
            jax.ShapeDtypeStruct((MLA_HEADS, n, MLA_ROW), _F32),
        ],
        compiler_params=_params(("parallel",)),
        name="l0_in",
    )(x, g, w_in, q_norm, w_uq, w_uk_bd, kv_norm, cos, sin)


def _pool_means(load_shifted, u, pos, group):
    outs = []
    for g, w in enumerate(POOL_WINDOWS):
        s = load_shifted(0, g)
        for k in range(1, w):
            s = s + load_shifted(k, g)
        cnt = jnp.minimum(pos + 1, w).astype(_F32)
        outs.append(s / cnt - u[:, g * group:(g + 1) * group])
    return outs


def _pool_seq_kernel(u_ref, buf_ref, w_ref, scale_ref, y_ref, ext_ref, *, pos0):
    t = pl.program_id(1)
    tt, width = u_ref.shape[1], u_ref.shape[2]
    group = width // len(POOL_WINDOWS)

    @pl.when(t == 0)
    def _():
        ext_ref[0:1, :] = jnp.zeros((1, width), _F32)
        ext_ref[1:POOL_HALO, :] = buf_ref[0]

    u = u_ref[0]
    ext_ref[POOL_HALO:POOL_HALO + tt, :] = u
    pos = pos0 + t * tt + lax.broadcasted_iota(jnp.int32, (tt, 1), 0)
    load = lambda k, g: ext_ref[POOL_HALO - k:POOL_HALO - k + tt, g * group:(g + 1) * group]
    pooled = _pool_means(load, u, pos, group)
    for g in range(len(POOL_WINDOWS)):
        mixed = _dot(_bf(pooled[g]), w_ref[g])
        y_ref[0, :, g * group:(g + 1) * group] = mixed * scale_ref[:, g * group:(g + 1) * group]
    ext_ref[0:POOL_HALO, :] = ext_ref[tt:tt + POOL_HALO, :]


def _pool_seq(u, buf, w, scale, *, tt, pos0):
    b, t, width = u.shape
    return pl.pallas_call(
        functools.partial(_pool_seq_kernel, pos0=pos0),
        grid=(b, t // tt),
        in_specs=[
            pl.BlockSpec((1, tt, width), lambda i, j: (i, j, 0)),
            pl.BlockSpec((1, POOL_BUF, width), lambda i, j: (i, 0, 0)),
            _const_spec(w.shape), _const_spec(scale.shape),
        ],
        out_specs=pl.BlockSpec((1, tt, width), lambda i, j: (i, j, 0)),
        out_shape=jax.ShapeDtypeStruct((b, t, width), _F32),
        scratch_shapes=[pltpu.VMEM((POOL_HALO + tt, width), _F32)],
        compiler_params=_params(("parallel", "arbitrary")),
        name="pool_seq",
    )(u, buf, w, scale)


def _pool_step_kernel(u_ref, st_ref, w_ref, scale_ref, y_ref, *, pos0):
    width = u_ref.shape[1]
    group = width // len(POOL_WINDOWS)
    u = u_ref[...]

    def load(k, g):
        cols = slice(g * group, (g + 1) * group)
        return u[:, cols] if k == 0 else st_ref[POOL_BUF - k, :, cols]

    pos = jnp.full((u.shape[0], 1), pos0, jnp.int32)
    pooled = _pool_means(load, u, pos, group)
    for g in range(len(POOL_WINDOWS)):
        mixed = _dot(_bf(pooled[g]), w_ref[g])
        y_ref[:, g * group:(g + 1) * group] = mixed * scale_ref[:, g * group:(g + 1) * group]


def _pool_step(u, st, w, scale, *, pos0):
    return pl.pallas_call(
        functools.partial(_pool_step_kernel, pos0=pos0),
        out_shape=jax.ShapeDtypeStruct(u.shape, _F32),
        compiler_params=pltpu.CompilerParams(vmem_limit_bytes=VMEM_LIMIT),
        name="pool_step",
    )(u, st, w, scale)


def _attn_kernel(qi_ref, ki_ref, q_ref, kv_ref, wuv_ref, o_ref, m_ref, l_ref, acc_ref, *, tq, tk):
    p = pl.program_id(1)
    qi = qi_ref[p]
    ki = ki_ref[p]
    rows = MLA_HEADS * tq

    @pl.when(ki == 0)
    def _():
        m_ref[...] = jnp.full(m_ref.shape, -jnp.inf, _F32)
        l_ref[...] = jnp.zeros(l_ref.shape, _F32)
        acc_ref[...] = jnp.zeros(acc_ref.shape, _F32)

    q = _bf(q_ref[...].reshape(rows, MLA_ROW))
    kv = _bf(kv_ref[...])
    s = _dot_nt(q, kv).reshape(MLA_HEADS, tq, tk)
    q_pos = qi * tq + lax.broadcasted_iota(jnp.int32, (1, tq, tk), 1)
    k_pos = ki * tk + lax.broadcasted_iota(jnp.int32, (1, tq, tk), 2)
    s = jnp.where(k_pos <= q_pos, s, -jnp.inf).reshape(rows, tk)
    m_old = m_ref[...]
    m_new = jnp.maximum(m_old, jnp.max(s, axis=-1, keepdims=True))
    alpha = jnp.exp(m_old - m_new)
    pr = jnp.exp(s - m_new)
    l_ref[...] = alpha * l_ref[...] + jnp.sum(pr, axis=-1, keepdims=True)
    acc_ref[...] = alpha * acc_ref[...] + _dot(_bf(pr), kv[:, :KV_LORA])
    m_ref[...] = m_new

    @pl.when(ki == ((qi + 1) * tq) // tk - 1)
    def _():
        o = _bf(acc_ref[...] / l_ref[...])
        for j in range(MLA_HEADS // 2):
            pair = jnp.concatenate([o[2 * j * tq:(2 * j + 1) * tq], o[(2 * j + 1) * tq:(2 * j + 2) * tq]], axis=-1)
            o_ref[:, j * 2 * V_HEAD:(j + 1) * 2 * V_HEAD] = _dot(pair, wuv_ref[j])


def _attention(q, rows, w_uv_bd, *, batch, seq, tq, tk):
    n = rows.shape[0]
    nq, nk = seq // tq, seq // tk
    pairs = [(i, j) for i in range(nq) for j in range(((i + 1) * tq) // tk)]
    qi_tab = jnp.asarray([p[0] for p in pairs], jnp.int32)
    ki_tab = jnp.asarray([p[1] for p in pairs], jnp.int32)
    grid_spec = pltpu.PrefetchScalarGridSpec(
        num_scalar_prefetch=2,
        grid=(batch, len(pairs)),
        in_specs=[
            pl.BlockSpec((MLA_HEADS, tq, MLA_ROW), lambda b, p, qi, ki: (0, b * nq + qi[p], 0)),
            pl.BlockSpec((tk, MLA_ROW), lambda b, p, qi, ki: (b * nk + ki[p], 0)),
            pl.BlockSpec(w_uv_bd.shape, lambda b, p, qi, ki: (0, 0, 0)),
        ],
        out_specs=pl.BlockSpec((tq, MLA_HEADS * V_HEAD), lambda b, p, qi, ki: (b * nq + qi[p], 0)),
        scratch_shapes=[
            pltpu.VMEM((MLA_HEADS * tq, 1), _F32),
            pltpu.VMEM((MLA_HEADS * tq, 1), _F32),
            pltpu.VMEM((MLA_HEADS * tq, KV_LORA), _F32),
        ],
    )
    return pl.pallas_call(
        functools.partial(_attn_kernel, tq=tq, tk=tk),
        grid_spec=grid_spec,
        out_shape=jax.ShapeDtypeStruct((n, MLA_HEADS * V_HEAD), _F32),
        compiler_params=_params(("parallel", "arbitrary")),
        name="attn_prompt",
    )(qi_tab, ki_tab, q, rows, w_uv_bd)


def _decode_attn_kernel(pt_ref, q_ref, row_ref, cache_ref, o_ref, buf_ref, sem_ref, m_ref, l_ref, acc_ref,
                        *, pages_per_step):
    b = pl.program_id(0)
    c = pl.program_id(1)
    nb = pl.num_programs(0)
    nc = pl.num_programs(1)
    step = b * nc + c
    slot = step % 2

    def page_copy(bb, cc, i, sl):
        page = pt_ref[bb, cc * pages_per_step + i]
        return pltpu.make_async_copy(cache_ref.at[page], buf_ref.at[sl, i], sem_ref.at[sl])

    def start_all(bb, cc, sl):
        def body(i, carry):
            page_copy(bb, cc, i, sl).start()
            return carry
        lax.fori_loop(0, pages_per_step, body, 0)

    @pl.when(step == 0)
    def _():
        start_all(b, c, slot)

    @pl.when(step + 1 < nb * nc)
    def _():
        nxt = step + 1
        start_all(nxt // nc, nxt % nc, 1 - slot)

    def wait_body(i, carry):
        page_copy(b, c, i, slot).wait()
        return carry
    lax.fori_loop(0, pages_per_step, wait_body, 0)

    @pl.when(c == 0)
    def _():
        m_ref[...] = jnp.full(m_ref.shape, -jnp.inf, _F32)
        l_ref[...] = jnp.zeros(l_ref.shape, _F32)
        acc_ref[...] = jnp.zeros(acc_ref.shape, _F32)

    q = _bf(q_ref[0])
    page_size = buf_ref.shape[2]
    kv = _bf(buf_ref[slot].reshape(pages_per_step * page_size, MLA_ROW))
    s = _dot_nt(q, kv)
    m_old = m_ref[...]
    m_new = jnp.maximum(m_old, jnp.max(s, axis=-1, keepdims=True))
    alpha = jnp.exp(m_old - m_new)
    pr = jnp.exp(s - m_new)
    l_ref[...] = alpha * l_ref[...] + jnp.sum(pr, axis=-1, keepdims=True)
    acc_ref[...] = alpha * acc_ref[...] + _dot(_bf(pr), kv[:, :KV_LORA])
    m_ref[...] = m_new

    @pl.when(c == nc - 1)
    def _():
        row = _bf(row_ref[0]).astype(_F32)
        s_new = jnp.sum(q.astype(_F32) * row, axis=-1, keepdims=True)
        m_old = m_ref[...]
        m_new = jnp.maximum(m_old, s_new)
        alpha = jnp.exp(m_old - m_new)
        p_new = jnp.exp(s_new - m_new)
        l_fin = alpha * l_ref[...] + p_new
        acc = alpha * acc_ref[...] + _bf(p_new).astype(_F32) * row[:, :KV_LORA]
        o_ref[0] = acc / l_fin


def _decode_attention(q, rows, cache, page_table):
    b, n_pages = page_table.shape
    page_size = cache.shape[1]
    pps = min(DECODE_PAGES_PER_STEP, n_pages)
    assert n_pages % pps == 0
    grid_spec = pltpu.PrefetchScalarGridSpec(
        num_scalar_prefetch=1,
        grid=(b, n_pages // pps),
        in_specs=[
            pl.BlockSpec((1, MLA_HEADS, MLA_ROW), lambda i, c, pt: (i, 0, 0)),
            pl.BlockSpec((1, 1, MLA_ROW), lambda i, c, pt: (i, 0, 0)),
            pl.BlockSpec(memory_space=pl.ANY),
        ],
        out_specs=pl.BlockSpec((1, MLA_HEADS, KV_LORA), lambda i, c, pt: (i, 0, 0)),
        scratch_shapes=[
            pltpu.VMEM((2, pps, page_size, MLA_ROW), _F32),
            pltpu.SemaphoreType.DMA((2,)),
            pltpu.VMEM((MLA_HEADS, 1), _F32),
            pltpu.VMEM((MLA_HEADS, 1), _F32),
            pltpu.VMEM((MLA_HEADS, KV_LORA), _F32),
        ],
    )
    return pl.pallas_call(
        functools.partial(_decode_attn_kernel, pages_per_step=pps),
        grid_spec=grid_spec,
        out_shape=jax.ShapeDtypeStruct((b, MLA_HEADS, KV_LORA), _F32),
        compiler_params=_params(("arbitrary", "arbitrary")),
        name="attn_decode",
    )(page_table, q, rows, cache)


def _uv_kernel(o_ref, wuv_ref, out_ref):
    for j in range(MLA_HEADS // 2):
        pair = _bf(o_ref[:, 2 * j * KV_LORA:(2 * j + 2) * KV_LORA])
        out_ref[:, j * 2 * V_HEAD:(j + 1) * 2 * V_HEAD] = _dot(pair, wuv_ref[j])


def _uv(o_lat, w_uv_bd):
    return pl.pallas_call(
        _uv_kernel,
        out_shape=jax.ShapeDtypeStruct((o_lat.shape[0], MLA_HEADS * V_HEAD), _F32),
        name="attn_uv",
    )(o_lat, w_uv_bd)


def _ple_tail(x, p_ref, gp_ref, wgate_ref, wproj_ref):
    gate = _sigmoid(_dot(_bf(_rms(x, gp_ref[...])), wgate_ref[...]))
    return x + gate * _dot(_bf(p_ref[...]), wproj_ref[...])


def _l0_tail_kernel(x_ref, py_ref, ao_ref, p_ref, wout_ref, gf_ref, wg_ref, wu_ref, wd_ref,
                    gp_ref, wgate_ref, wproj_ref, o_ref, x1_ref, h_ref, acc_ref):
    j = pl.program_id(1)
    half = py_ref.shape[1]

    @pl.when(j == 0)
    def _():
        mix = _dot(_bf(py_ref[...]), wout_ref[:half, :]) + _dot(_bf(ao_ref[...]), wout_ref[half:, :])
        x1 = x_ref[...] + mix
        x1_ref[...] = x1
        h_ref[...] = _bf(_rms(x1, gf_ref[...]))
        acc_ref[...] = jnp.zeros(acc_ref.shape, _F32)

    h = h_ref[...]
    act = _silu(_dot(h, wg_ref[...])) * _dot(h, wu_ref[...])
    acc_ref[...] += _dot(_bf(act), wd_ref[...])

    @pl.when(j == pl.num_programs(1) - 1)
    def _():
        o_ref[...] = _ple_tail(x1_ref[...] + acc_ref[...], p_ref, gp_ref, wgate_ref, wproj_ref)


def _l0_tail(x, pool_y, attn_o, p, w_out, g_ffn, w_g, w_u, w_d, g_ple, w_gate, w_proj, *, tm, tf):
    n, d = x.shape
    d_ff = w_g.shape[1]
    row = lambda i, j: (i, 0)
    return pl.pallas_call(
        _l0_tail_kernel,
        grid=(n // tm, d_ff // tf),
        in_specs=[
            pl.BlockSpec((tm, d), row), pl.BlockSpec((tm, pool_y.shape[1]), row),
            pl.BlockSpec((tm, attn_o.shape[1]), row), pl.BlockSpec((tm, p.shape[1]), row),
            _const_spec(w_out.shape), _const_spec(g_ffn.shape),
            pl.BlockSpec((d, tf), lambda i, j: (0, j)), pl.BlockSpec((d, tf), lambda i, j: (0, j)),
            pl.BlockSpec((tf, d), lambda i, j: (j, 0)),
            _const_spec(g_ple.shape), _const_spec(w_gate.shape), _const_spec(w_proj.shape),
        ],
        out_specs=pl.BlockSpec((tm, d), row),
        out_shape=jax.ShapeDtypeStruct((n, d), _F32),
        scratch_shapes=[pltpu.VMEM((tm, d), _F32), pltpu.VMEM((tm, d), _BF16), pltpu.VMEM((tm, d), _F32)],
        compiler_params=_params(("parallel", "arbitrary")),
        name="l0_tail",
    )(x, pool_y, attn_o, p, w_out, g_ffn, w_g, w_u, w_d, g_ple, w_gate, w_proj)


def _route_top2(logits, n_experts):
    lane = lax.broadcasted_iota(jnp.int32, logits.shape, 1)
    lg = jnp.where(lane < n_experts, logits, -jnp.inf)
    m1 = jnp.max(lg, axis=-1, keepdims=True)
    i1 = jnp.min(jnp.where(lg == m1, lane, LANES), axis=-1, keepdims=True)
    lg2 = jnp.where(lane == i1, -jnp.inf, lg)
    m2 = jnp.max(lg2, axis=-1, keepdims=True)
    i2 = jnp.min(jnp.where(lg2 == m2, lane, LANES), axis=-1, keepdims=True)
    e2 = jnp.exp(m2 - m1)
    den = 1.0 + e2
    return jnp.where(lane == i1, 1.0 / den, 0.0) + jnp.where(lane == i2, e2 / den, 0.0)


def _l1_tail_kernel(x_ref, y_ref, p_ref, wout_ref, gf_ref, wr_ref, wg_ref, wu_ref, wd_ref,
                    gp_ref, wgate_ref, wproj_ref, gfin_ref, o_ref, x1_ref, h_ref, comb_ref, acc_ref,
                    *, n_experts):
    e = pl.program_id(1)

    @pl.when(e == 0)
    def _():
        x1 = x_ref[...] + _dot(_bf(y_ref[...]), wout_ref[...])
        x1_ref[...] = x1
        h = _bf(_rms(x1, gf_ref[...]))
        h_ref[...] = h
        comb_ref[...] = _route_top2(_dot(h, wr_ref[...]), n_experts)
        acc_ref[...] = jnp.zeros(acc_ref.shape, _F32)

    h = h_ref[...]
    act = _silu(_dot(h, wg_ref[0])) * _dot(h, wu_ref[0])
    lane = lax.broadcasted_iota(jnp.int32, comb_ref.shape, 1)
    c_e = jnp.sum(jnp.where(lane == e, comb_ref[...], 0.0), axis=-1, keepdims=True)
    acc_ref[...] += c_e * _dot(_bf(act), wd_ref[0])

    @pl.when(e == n_experts - 1)
    def _():
        x3 = _ple_tail(x1_ref[...] + acc_ref[...], p_ref, gp_ref, wgate_ref, wproj_ref)
        o_ref[...] = _rms(x3, gfin_ref[...])


def _l1_tail(x, y, p, w_out, g_ffn, w_router, w_g, w_u, w_d, g_ple, w_gate, w_proj, g_final, *, tm, n_experts):
    n, d = x.shape
    d_e = w_g.shape[2]
    row = lambda i, e: (i, 0)
    return pl.pallas_call(
        functools.partial(_l1_tail_kernel, n_experts=n_experts),
        grid=(n // tm, n_experts),
        in_specs=[
            pl.BlockSpec((tm, d), row), pl.BlockSpec((tm, y.shape[1]), row), pl.BlockSpec((tm, p.shape[1]), row),
            _const_spec(w_out.shape), _const_spec(g_ffn.shape), _const_spec(w_router.shape),
            pl.BlockSpec((1, d, d_e), lambda i, e: (e, 0, 0)), pl.BlockSpec((1, d, d_e), lambda i, e: (e, 0, 0)),
            pl.BlockSpec((1, d_e, d), lambda i, e: (e, 0, 0)),
            _const_spec(g_ple.shape), _const_spec(w_gate.shape), _const_spec(w_proj.shape),
            _const_spec(g_final.shape),
        ],
        out_specs=pl.BlockSpec((tm, d), row),
        out_shape=jax.ShapeDtypeStruct((n, d), _F32),
        scratch_shapes=[pltpu.VMEM((tm, d), _F32), pltpu.VMEM((tm, d), _BF16),
                        pltpu.VMEM((tm, LANES), _F32), pltpu.VMEM((tm, d), _F32)],
        compiler_params=_params(("parallel", "arbitrary")),
        name="l1_tail",
    )(x, y, p, w_out, g_ffn, w_router, w_g, w_u, w_d, g_ple, w_gate, w_proj, g_final)


def _l1_in_kernel(x_ref, g_ref, w_ref, xb_ref, gg_ref):
    d_rnn = xb_ref.shape[1]
    proj = _dot(_bf(_rms(x_ref[...], g_ref[...])), w_ref[...])
    xb_ref[...] = proj[:, :d_rnn]
    gg_ref[...] = _gelu_tanh(proj[:, d_rnn:])


def _l1_in(x, g, w, *, tm):
    n, d = x.shape
    d_rnn = w.shape[1] // 2
    row = lambda i: (i, 0)
    return pl.pallas_call(
        _l1_in_kernel,
        grid=(n // tm,),
        in_specs=[pl.BlockSpec((tm, d), row), _const_spec(g.shape), _const_spec(w.shape)],
        out_specs=[pl.BlockSpec((tm, d_rnn), row), pl.BlockSpec((tm, d_rnn), row)],
        out_shape=[jax.ShapeDtypeStruct((n, d_rnn), _F32), jax.ShapeDtypeStruct((n, d_rnn), _F32)],
        compiler_params=_params(("parallel",)),
        name="l1_in",
    )(x, g, w)


def _lru_gates(xc, wrg_ref, brg_ref, wig_ref, big_ref, lam_ref):
    xcb = _bf(xc)
    rs, igs = [], []
    for j in range(wrg_ref.shape[0]):
        blk = xcb[:, j * 2 * LANES:(j + 1) * 2 * LANES]
        rs.append(_dot(blk, wrg_ref[j]))
        igs.append(_dot(blk, wig_ref[j]))
    r = _sigmoid(jnp.concatenate(rs, axis=-1) + brg_ref[...])
    ig = _sigmoid(jnp.concatenate(igs, axis=-1) + big_ref[...])
    log_a = (-LRU_C * r) * _softplus(-lam_ref[...])
    a = jnp.exp(log_a)
    th = jnp.tanh(log_a)
    mult = jnp.sqrt((-2.0 * th) / (1.0 - th))
    return a, mult * (ig * xc)


def _rglru_seq_kernel(xb_ref, gg_ref, cbuf_ref, h0_ref, cw_ref, cb_ref, wrg_ref, brg_ref, wig_ref, big_ref,
                      lam_ref, y_ref, hl_ref, ext_ref, a_ref, b_ref, hc_ref):
    t = pl.program_id(1)
    tt, d = xb_ref.shape[1], xb_ref.shape[2]
    nbuf = CONV_WIDTH - 1

    @pl.when(t == 0)
    def _():
        ext_ref[0:CONV_HALO - nbuf, :] = jnp.zeros((CONV_HALO - nbuf, d), _F32)
        ext_ref[CONV_HALO - nbuf:CONV_HALO, :] = cbuf_ref[0]
        hc_ref[...] = h0_ref[0]

    ext_ref[CONV_HALO:CONV_HALO + tt, :] = xb_ref[0]
    xc = cb_ref[...] + ext_ref[CONV_HALO - nbuf:CONV_HALO - nbuf + tt, :] * cw_ref[0:1, :]
    for k in range(1, CONV_WIDTH):
        xc = xc + ext_ref[CONV_HALO - nbuf + k:CONV_HALO - nbuf + k + tt, :] * cw_ref[k:k + 1, :]
    a, b = _lru_gates(xc, wrg_ref, brg_ref, wig_ref, big_ref, lam_ref)
    a_ref[...] = a
    b_ref[...] = b
    row = lax.broadcasted_iota(jnp.int32, (SUBLANES, d), 0)

    def group(g, hc):
        r0 = pl.multiple_of(g * SUBLANES, SUBLANES)
        aa = a_ref[pl.ds(r0, SUBLANES), :]
        bb = b_ref[pl.ds(r0, SUBLANES), :]
        for sh in (1, 2, 4):
            keep = row >= sh
            a_s = jnp.where(keep, pltpu.roll(aa, sh, 0), 1.0)
            b_s = jnp.where(keep, pltpu.roll(bb, sh, 0), 0.0)
            bb = aa * b_s + bb
            aa = aa * a_s
        h = aa * hc + bb
        y_ref[0, pl.ds(r0, SUBLANES), :] = h * gg_ref[0, pl.ds(r0, SUBLANES), :]
        return h[SUBLANES - 1:SUBLANES, :]

    hc = lax.fori_loop(0, tt // SUBLANES, group, hc_ref[...])
    hc_ref[...] = hc
    hl_ref[0] = hc
    ext_ref[0:CONV_HALO, :] = ext_ref[tt:tt + CONV_HALO, :]


def _rglru_seq(xb, gg, cbuf, h0, cw, cb, wrg, brg, wig, big, lam, *, tt):
    b, t, d = xb.shape
    seq = lambda i, j: (i, j, 0)
    per_b = lambda i, j: (i, 0, 0)
    return pl.pallas_call(
        _rglru_seq_kernel,
        grid=(b, t // tt),
        in_specs=[
            pl.BlockSpec((1, tt, d), seq), pl.BlockSpec((1, tt, d), seq),
            pl.BlockSpec((1, CONV_WIDTH - 1, d), per_b), pl.BlockSpec((1, 1, d), per_b),
            _const_spec(cw.shape), _const_spec(cb.shape), _const_spec(wrg.shape), _const_spec(brg.shape),
            _const_spec(wig.shape), _const_spec(big.shape), _const_spec(lam.shape),
        ],
        out_specs=[pl.BlockSpec((1, tt, d), seq), pl.BlockSpec((1, 1, d), per_b)],
        out_shape=[jax.ShapeDtypeStruct((b, t, d), _F32), jax.ShapeDtypeStruct((b, 1, d), _F32)],
        scratch_shapes=[pltpu.VMEM((CONV_HALO + tt, d), _F32), pltpu.VMEM((tt, d), _F32),
                        pltpu.VMEM((tt, d), _F32), pltpu.VMEM((1, d), _F32)],
        compiler_params=_params(("parallel", "arbitrary")),
        name="rglru_seq",
    )(xb, gg, cbuf, h0, cw, cb, wrg, brg, wig, big, lam)


def _rglru_step_kernel(xb_ref, gg_ref, cst_ref, h0_ref, cw_ref, cb_ref, wrg_ref, brg_ref, wig_ref, big_ref,
                       lam_ref, y_ref, h_ref):
    xc = cb_ref[...] + xb_ref[...] * cw_ref[CONV_WIDTH - 1:CONV_WIDTH, :]
    for k in range(CONV_WIDTH - 1):
        xc = xc + cst_ref[k] * cw_ref[k:k + 1, :]
    a, b = _lru_gates(xc, wrg_ref, brg_ref, wig_ref, big_ref, lam_ref)
    h = a * h0_ref[...] + b
    h_ref[...] = h
    y_ref[...] = h * gg_ref[...]


def _rglru_step(xb, gg, cst, h0, cw, cb, wrg, brg, wig, big, lam):
    return pl.pallas_call(
        _rglru_step_kernel,
        out_shape=[jax.ShapeDtypeStruct(xb.shape, _F32), jax.ShapeDtypeStruct(xb.shape, _F32)],
        compiler_params=pltpu.CompilerParams(vmem_limit_bytes=VMEM_LIMIT),
        name="rglru_step",
    )(xb, gg, cst, h0, cw, cb, wrg, brg, wig, big, lam)


def _pair_block_diag(blocks):
    n2, r, c = blocks.shape
    b = blocks.reshape(n2 // 2, 2, r, c)
    z = jnp.zeros((n2 // 2, r, c), blocks.dtype)
    top = jnp.concatenate([b[:, 0], z], axis=-1)
    bot = jnp.concatenate([z, b[:, 1]], axis=-1)
    return jnp.concatenate([top, bot], axis=1)


def _rope_tables(pos):
    inv = ROPE_THETA ** (-jnp.arange(0, QK_ROPE, 2, dtype=_F32) / QK_ROPE)
    ang = pos.astype(_F32)[:, None] * inv[None, :]
    cos, sin = jnp.cos(ang), jnp.sin(ang)
    reps = 2 * LANES // QK_ROPE
    return (jnp.tile(jnp.concatenate([cos, cos], axis=-1), (1, reps)),
            jnp.tile(jnp.concatenate([-sin, sin], axis=-1), (1, reps)))


def _prep_weights(prm):
    w = {}
    row2 = lambda v: v.reshape(1, -1)
    w["norm_mix"] = [row2(prm["norm_mix"][i]) for i in range(2)]
    w["norm_ffn"] = [row2(prm["norm_ffn"][i]) for i in range(2)]
    w["norm_ple"] = [row2(prm["norm_ple"][i]) for i in range(2)]
    w["norm_final"] = row2(prm["norm_final"])
    w_in0 = prm["w_in0"][0]
    assert (w_in0.shape[1] - QK_ROPE) % LANES == 0, "rotary key columns must start on a lane-tile boundary"
    w["w_in0"] = _bf(jnp.pad(w_in0, ((0, 0), (0, LANES - QK_ROPE))))
    w["pool_w"] = _bf(prm["pool_w"][0])
    w["pool_scale"] = row2(prm["pool_scale"][0])
    w["q_norm"] = row2(prm["q_norm"][0])
    w["kv_norm"] = row2(prm["kv_norm"][0])
    qh = QK_NOPE + QK_ROPE
    cols = np.concatenate([np.concatenate([h * qh + np.arange(QK_NOPE) for h in range(MLA_HEADS)]),
                           np.concatenate([h * qh + QK_NOPE + np.arange(QK_ROPE) for h in range(MLA_HEADS)])])
    w["w_uq"] = _bf(prm["w_uq"][0][:, cols])
    w["w_uk_bd"] = _bf(_pair_block_diag(jnp.transpose(prm["w_uk"][0], (1, 2, 0))))
    w["w_uv_bd"] = _bf(_pair_block_diag(jnp.transpose(prm["w_uv"][0], (1, 0, 2))))
    w["w_out0"] = _bf(prm["w_out0"][0])
    w["w_ffn_gate"] = _bf(prm["w_ffn_gate"][0])
    w["w_ffn_up"] = _bf(prm["w_ffn_up"][0])
    w["w_ffn_down"] = _bf(prm["w_ffn_down"][0])
    w["w_ple_gate"] = [_bf(prm["w_ple_gate"][i]) for i in range(2)]
    w["w_ple_proj"] = [_bf(prm["w_ple_proj"][i]) for i in range(2)]
    w["w_in1"] = _bf(prm["w_in1"][0])
    w["conv_w"] = prm["conv_w"][0]
    w["conv_b"] = row2(prm["conv_b"][0])
    w["w_rg_bd"] = _bf(_pair_block_diag(prm["w_rg"][0]))
    w["w_ig_bd"] = _bf(_pair_block_diag(prm["w_ig"][0]))
    w["b_rg"] = row2(prm["b_rg"][0])
    w["b_ig"] = row2(prm["b_ig"][0])
    w["lru_lambda"] = row2(prm["lru_lambda"][0])
    w["w_out1"] = _bf(prm["w_out1"][0])
    n_exp = prm["w_router"].shape[2]
    w["w_router"] = _bf(jnp.pad(prm["w_router"][0], ((0, 0), (0, LANES - n_exp))))
    w["n_experts"] = n_exp
    w["w_exp_gate"] = _bf(prm["w_exp_gate"][0])
    w["w_exp_up"] = _bf(prm["w_exp_up"][0])
    w["w_exp_down"] = _bf(prm["w_exp_down"][0])
    return w


def _pick_tile(n, target):
    t = min(n, target)
    while n % t:
        t //= 2
    return t


def _run_group(x, p, pos0, pool_buf, conv_buf, lru_h, cache, page_table, w):
    bsz, t, d = x.shape
    n = bsz * t
    decode = cache is not None
    xf = x.reshape(n, d)
    pf = p.reshape(p.shape[0], n, p.shape[-1])
    tm = _pick_tile(n, 512)
    pool_width = w["pool_scale"].shape[1]

    tab_len = max(t, tm)
    pos = pos0 + (jnp.arange(tab_len, dtype=jnp.int32) % t)
    cos, sin = _rope_tables(pos)
    u, rows, q = _l0_in(xf, w["norm_mix"][0], w["w_in0"], w["q_norm"], w["w_uq"], w["w_uk_bd"], w["kv_norm"],
                        cos, sin, tm=tm, pool_width=pool_width)
    if decode:
        assert t == 1
        pool_y = _pool_step(u, jnp.transpose(pool_buf, (1, 0, 2)), w["pool_w"], w["pool_scale"], pos0=pos0)
        o_lat = _decode_attention(jnp.transpose(q, (1, 0, 2)), rows.reshape(n, 1, MLA_ROW), cache, page_table)
        attn_o = _uv(o_lat.reshape(n, MLA_HEADS * KV_LORA), w["w_uv_bd"])
        new_pool = jnp.concatenate([pool_buf[:, 1:], u.reshape(bsz, 1, pool_width)], axis=1)
    else:
        u3 = u.reshape(bsz, t, pool_width)
        pool_y = _pool_seq(u3, pool_buf, w["pool_w"], w["pool_scale"], tt=_pick_tile(t, 512), pos0=pos0)
        pool_y = pool_y.reshape(n, pool_width)
        attn_o = _attention(q, rows, w["w_uv_bd"], batch=bsz, seq=t, tq=_pick_tile(t, 256), tk=_pick_tile(t, 256))
        assert t >= POOL_BUF and t >= CONV_WIDTH - 1
        new_pool = u3[:, t - POOL_BUF:]
    x1 = _l0_tail(xf, pool_y, attn_o, pf[0], w["w_out0"], w["norm_ffn"][0], w["w_ffn_gate"], w["w_ffn_up"],
                  w["w_ffn_down"], w["norm_ple"][0], w["w_ple_gate"][0], w["w_ple_proj"][0],
                  tm=tm, tf=w["w_ffn_gate"].shape[1] // 2)

    xb, gg = _l1_in(x1, w["norm_mix"][1], w["w_in1"], tm=tm)
    d_rnn = xb.shape[1]
    lru_args = (w["conv_w"], w["conv_b"], w["w_rg_bd"], w["b_rg"], w["w_ig_bd"], w["b_ig"], w["lru_lambda"])
    if decode:
        y, h_last = _rglru_step(xb, gg, jnp.transpose(conv_buf, (1, 0, 2)), lru_h, *lru_args)
        new_conv = jnp.concatenate([conv_buf[:, 1:], xb.reshape(bsz, 1, d_rnn)], axis=1)
    else:
        xb3 = xb.reshape(bsz, t, d_rnn)
        y, h_last = _rglru_seq(xb3, gg.reshape(bsz, t, d_rnn), conv_buf, lru_h.reshape(bsz, 1, d_rnn), *lru_args,
                               tt=_pick_tile(t, 256))
        y = y.reshape(n, d_rnn)
        h_last = h_last.reshape(bsz, d_rnn)
        new_conv = xb3[:, t - (CONV_WIDTH - 1):]
    out = _l1_tail(x1, y, pf[1], w["w_out1"], w["norm_ffn"][1], w["w_router"], w["w_exp_gate"], w["w_exp_up"],
                   w["w_exp_down"], w["norm_ple"][1], w["w_ple_gate"][1], w["w_ple_proj"][1], w["norm_final"],
                   tm=tm, n_experts=w["n_experts"])
    return (out.reshape(bsz, t, d), rows.reshape(1, bsz, t, MLA_ROW), new_pool[None], new_conv[None], h_last[None])


def kernel(x_prompt, x_sample, p_prompt, p_sample, cache_mla, state_pool, state_conv, state_lru, page_table, norm_mix, norm_ffn, norm_ple, norm_final, w_in0, pool_w, pool_scale, q_norm, kv_norm, w_uq, w_uk, w_uv, w_out0, w_in1, conv_w, conv_b, w_rg, b_rg, w_ig, b_ig, lru_lambda, w_out1, w_ffn_gate, w_ffn_up, w_ffn_down, w_router, w_exp_gate, w_exp_up, w_exp_down, w_ple_proj, w_ple_gate):
    assert norm_mix.shape[0] == 2, "two trunk layers: pool/MLA then RG-LRU/MoE"
    prm = dict(norm_mix=norm_mix, norm_ffn=norm_ffn, norm_ple=norm_ple, norm_final=norm_final,
               w_in0=w_in0, pool_w=pool_w, pool_scale=pool_scale, q_norm=q_norm, kv_norm=kv_norm,
               w_uq=w_uq, w_uk=w_uk, w_uv=w_uv, w_out0=w_out0, w_in1=w_in1, conv_w=conv_w,
               conv_b=conv_b, w_rg=w_rg, b_rg=b_rg, w_ig=w_ig, b_ig=b_ig, lru_lambda=lru_lambda,
               w_out1=w_out1, w_ffn_gate=w_ffn_gate, w_ffn_up=w_ffn_up, w_ffn_down=w_ffn_down,
               w_router=w_router, w_exp_gate=w_exp_gate, w_exp_up=w_exp_up,
               w_exp_down=w_exp_down, w_ple_proj=w_ple_proj, w_ple_gate=w_ple_gate)
    w = _prep_weights(prm)
    bsz = x_prompt.shape[0]
    dt = x_prompt.dtype
    past_len = page_table.shape[1] * cache_mla.shape[2]
    zero_pool = jnp.zeros((bsz, POOL_BUF, state_pool.shape[-1]), dt)
    zero_conv = jnp.zeros((bsz, CONV_WIDTH - 1, state_conv.shape[-1]), dt)
    zero_h = jnp.zeros((bsz, state_lru.shape[-1]), dt)
    y_p, rows_p, pool_p, conv_p, h_p = _run_group(
        x_prompt, p_prompt, 0, zero_pool, zero_conv, zero_h, None, None, w)
    y_s, rows_s, pool_s, conv_s, h_s = _run_group(
        x_sample, p_sample, past_len, state_pool[0], state_conv[0], state_lru[0],
        cache_mla.reshape(cache_mla.shape[1:]), page_table, w)
    return (y_p, y_s, rows_p, rows_s, pool_p, pool_s, conv_p, conv_s, h_p, h_s)
```

```python
import functools

import numpy as np
import jax
import jax.numpy as jnp
from jax import lax
from jax.experimental import pallas as pl
from jax.experimental.pallas import tpu as pltpu

NORM_EPS = 1e-6
ROPE_THETA = 10000.0
LRU_C = 8.0
POOL_WINDOWS = (2, 4, 8, 16)
POOL_BUF = max(POOL_WINDOWS) - 1
POOL_HALO = 16
CONV_WIDTH = 4
CONV_HALO = 8
MLA_HEADS = 8
QK_NOPE = 64
QK_ROPE = 32
V_HEAD = 64
KV_LORA = 128
MLA_ROW = KV_LORA + QK_ROPE
MLA_SCALE = (QK_NOPE + QK_ROPE) ** -0.5
Q_SCALE = MLA_SCALE * float(np.log2(np.e))
TOP_K = 2
LANES = 128
SUBLANES = 8
DECODE_PAGES_PER_STEP = 32
VMEM_LIMIT = 56 * 1024 * 1024

_F32 = jnp.float32
_BF16 = jnp.bfloat16


def _bf(x):
    return x.astype(_BF16)


def _dot(a, b):
    return jnp.dot(a, b, preferred_element_type=_F32)


def _dot_nt(a, b):
    return lax.dot_general(a, b, (((1,), (1,)), ((), ())), preferred_element_type=_F32)


def _rms(x, g):
    ms = jnp.mean(x * x, axis=-1, keepdims=True)
    return (x * lax.rsqrt(ms + NORM_EPS)) * g


def _sigmoid(x):
    return 1.0 / (1.0 + jnp.exp(-x))


def _silu(x):
    return x * _sigmoid(x)


def _gelu_tanh(x):
    c = np.float32(np.sqrt(2.0 / np.pi))
    return 0.5 * x * (1.0 + jnp.tanh(c * (x + 0.044715 * (x * x * x))))


def _softplus(x):
    return jnp.maximum(x, 0.0) + jnp.log1p(jnp.exp(-jnp.abs(x)))


def _params(sem):
    return pltpu.CompilerParams(dimension_semantics=sem, vmem_limit_bytes=VMEM_LIMIT)


def _const_spec(shape):
    n = len(shape)
    return pl.BlockSpec(shape, lambda *_: (0,) * n)


def _swap_halves(v, first_half):
    return jnp.where(first_half, pltpu.roll(v, LANES - 16, 1), pltpu.roll(v, 16, 1))


def _l0_in_kernel(x_ref, g_ref, win_ref, qn_ref, wuq_ref, wuk_ref, kvn_ref, cos_ref, sin_ref,
                  u_ref, rows_ref, q_ref, kvb_ref, latT_ref):
    tm = x_ref.shape[0]
    pool_w = u_ref.shape[1]
    xn = _rms(x_ref[...], g_ref[...])
    proj = _dot(_bf(xn), win_ref[...])
    u_ref[...] = proj[:, :pool_w]
    cos = cos_ref[...]
    sin = sin_ref[...]
    lane = lax.broadcasted_iota(jnp.int32, (tm, LANES), 1)
    first_half = (lane % QK_ROPE) < (QK_ROPE // 2)
    o_q = pool_w
    o_kv = o_q + wuq_ref.shape[0]
    o_kr = o_kv + KV_LORA
    lat = _rms(proj[:, o_kv:o_kr], kvn_ref[...])
    kr = proj[:, o_kr:o_kr + LANES]
    kr = (kr * cos[:, :LANES] + _swap_halves(kr, first_half) * sin[:, :LANES])[:, :QK_ROPE]
    rows_ref[:, :KV_LORA] = lat
    rows_ref[:, KV_LORA:MLA_ROW] = kr
    kvb_ref[:, :KV_LORA] = _bf(lat)
    kvb_ref[:, KV_LORA:MLA_ROW] = _bf(kr)
    latT_ref[...] = _bf(lat.T)
    q = _dot(_bf(_rms(proj[:, o_q:o_kv], qn_ref[...])), wuq_ref[...])
    n_nope = MLA_HEADS * QK_NOPE
    heads_per_col = LANES // QK_ROPE
    for c in range(MLA_HEADS // heads_per_col):
        qr = q[:, n_nope + c * LANES:n_nope + (c + 1) * LANES]
        cs = cos[:, c * LANES:(c + 1) * LANES]
        sn = sin[:, c * LANES:(c + 1) * LANES]
        qrot = (qr * cs + _swap_halves(qr, first_half) * sn) * Q_SCALE
        for k in range(heads_per_col):
            sh = qrot if k == 0 else pltpu.roll(qrot, LANES - QK_ROPE * k, 1)
            q_ref[c * heads_per_col + k, :, KV_LORA:MLA_ROW] = _bf(sh[:, :QK_ROPE])
    for j in range(MLA_HEADS // 2):
        lat2 = _dot(_bf(q[:, j * LANES:(j + 1) * LANES]), wuk_ref[j]) * Q_SCALE
        q_ref[2 * j, :, :KV_LORA] = _bf(lat2[:, :KV_LORA])
        q_ref[2 * j + 1, :, :KV_LORA] = _bf(lat2[:, KV_LORA:])


def _l0_in(x, g, w_in, q_norm, w_uq, w_uk_bd, kv_norm, cos, sin, *, tm, pool_width):
    n, d = x.shape
    n_tab = cos.shape[0] // tm
    row = lambda i: (i, 0)
    tab = lambda i: (i % n_tab, 0)
    return pl.pallas_call(
        _l0_in_kernel,
        grid=(n // tm,),
        in_specs=[
            pl.BlockSpec((tm, d), row),
            _const_spec(g.shape), _const_spec(w_in.shape), _const_spec(q_norm.shape),
            _const_spec(w_uq.shape), _const_spec(w_uk_bd.shape), _const_spec(kv_norm.shape),
            pl.BlockSpec((tm, cos.shape[1]), tab), pl.BlockSpec((tm, sin.shape[1]), tab),
        ],
        out_specs=[
            pl.BlockSpec((tm, pool_width), row),
            pl.BlockSpec((tm, MLA_ROW), row),
            pl.BlockSpec((MLA_HEADS, tm, MLA_ROW), lambda i: (0, i, 0)),
            pl.BlockSpec((tm, MLA_ROW), row),
            pl.BlockSpec((KV_LORA, tm), lambda i: (0, i)),
        ],
        out_shape=[
            jax.ShapeDtypeStruct((n, pool_width), _F32),
            jax.ShapeDtypeStruct((n, MLA_ROW), _F32),
            jax.ShapeDtypeStruct((MLA_HEADS, n, MLA_ROW), _BF16),
            jax.ShapeDtypeStruct((n, MLA_ROW), _BF16),
            jax.ShapeDtypeStruct((KV_LORA, n), _BF16),
        ],
        compiler_params=_params(("parallel",)),
        name="l0_in",
    )(x, g, w_in, q_norm, w_uq, w_uk_bd, kv_norm, cos, sin)


def _pool_means(load_shifted, u, pos, group):
    outs = []
    for g, w in enumerate(POOL_WINDOWS):
        s = load_shifted(0, g)
        for k in range(1, w):
            s = s + load_shifted(k, g)
        cnt = jnp.minimum(pos + 1, w).astype(_F32)
        outs.append(s / cnt - u[:, g * group:(g + 1) * group])
    return outs


def _pool_seq_kernel(u_ref, buf_ref, w_ref, scale_ref, y_ref, ext_ref, *, pos0):
    t = pl.program_id(1)
    tt, width = u_ref.shape[1], u_ref.shape[2]
    group = width // len(POOL_WINDOWS)

    @pl.when(t == 0)
    def _():
        ext_ref[0:1, :] = jnp.zeros((1, width), _F32)
        ext_ref[1:POOL_HALO, :] = buf_ref[0]

    u = u_ref[0]
    ext_ref[POOL_HALO:POOL_HALO + tt, :] = u
    pos = pos0 + t * tt + lax.broadcasted_iota(jnp.int32, (tt, 1), 0)
    load = lambda k, g: ext_ref[POOL_HALO - k:POOL_HALO - k + tt, g * group:(g + 1) * group]
    pooled = _pool_means(load, u, pos, group)
    for g in range(len(POOL_WINDOWS)):
        mixed = _dot(_bf(pooled[g]), w_ref[g])
        y_ref[0, :, g * group:(g + 1) * group] = mixed * scale_ref[:, g * group:(g + 1) * group]
    ext_ref[0:POOL_HALO, :] = ext_ref[tt:tt + POOL_HALO, :]


def _pool_seq(u, buf, w, scale, *, tt, pos0):
    b, t, width = u.shape
    return pl.pallas_call(
        functools.partial(_pool_seq_kernel, pos0=pos0),
        grid=(b, t // tt),
        in_specs=[
            pl.BlockSpec((1, tt, width), lambda i, j: (i, j, 0)),
            pl.BlockSpec((1, POOL_BUF, width), lambda i, j: (i, 0, 0)),
            _const_spec(w.shape), _const_spec(scale.shape),
        ],
        out_specs=pl.BlockSpec((1, tt, width), lambda i, j: (i, j, 0)),
        out_shape=jax.ShapeDtypeStruct((b, t, width), _F32),
        scratch_shapes=[pltpu.VMEM((POOL_HALO + tt, width), _F32)],
        compiler_params=_params(("parallel", "arbitrary")),
        name="pool_seq",
    )(u, buf, w, scale)


def _pool_step_kernel(u_ref, st_ref, w_ref, scale_ref, y_ref, *, pos0):
    width = u_ref.shape[1]
    group = width // len(POOL_WINDOWS)
    u = u_ref[...]

    def load(k, g):
        cols = slice(g * group, (g + 1) * group)
        return u[:, cols] if k == 0 else st_ref[POOL_BUF - k, :, cols]

    pos = jnp.full((u.shape[0], 1), pos0, jnp.int32)
    pooled = _pool_means(load, u, pos, group)
    for g in range(len(POOL_WINDOWS)):
        mixed = _dot(_bf(pooled[g]), w_ref[g])
        y_ref[:, g * group:(g + 1) * group] = mixed * scale_ref[:, g * group:(g + 1) * group]


def _pool_step(u, st, w, scale, *, pos0):
    return pl.pallas_call(
        functools.partial(_pool_step_kernel, pos0=pos0),
        out_shape=jax.ShapeDtypeStruct(u.shape, _F32),
        compiler_params=pltpu.CompilerParams(vmem_limit_bytes=VMEM_LIMIT),
        name="pool_step",
    )(u, st, w, scale)


def _attn_kernel(qi_ref, ki_ref, q_ref, kv_ref, latT_ref, wuv_ref, o_ref, m_ref, l_ref, acc_ref, *, tq, tk):
    p = pl.program_id(1)
    qi = qi_ref[p]
    ki = ki_ref[p]
    cols = MLA_HEADS * tq
    last = ((qi + 1) * tq - 1) // tk

    @pl.when(ki == 0)
    def _():
        m_ref[...] = jnp.full(m_ref.shape, -jnp.inf, _F32)
        l_ref[...] = jnp.zeros(l_ref.shape, _F32)
        acc_ref[...] = jnp.zeros(acc_ref.shape, _F32)

    q = q_ref[...].reshape(cols, MLA_ROW)

    def step(masked):
        s = _dot_nt(kv_ref[...], q)
        if masked:
            k_pos = ki * tk + lax.broadcasted_iota(jnp.int32, (tk, tq), 0)
            q_pos = qi * tq + lax.broadcasted_iota(jnp.int32, (tk, tq), 1)
            keep = jnp.concatenate([k_pos <= q_pos] * MLA_HEADS, axis=1)
            s = jnp.where(keep, s, -jnp.inf)
        m_old = m_ref[...]
        m_new = jnp.maximum(m_old, jnp.max(s, axis=0, keepdims=True))
        alpha = jnp.exp2(m_old - m_new)
        pr = jnp.exp2(s - m_new)
        l_ref[...] = alpha * l_ref[...] + jnp.sum(pr, axis=0, keepdims=True)
        acc_ref[...] = alpha * acc_ref[...] + _dot(latT_ref[...], _bf(pr))
        m_ref[...] = m_new

    @pl.when(ki < last)
    def _():
        step(False)

    @pl.when(ki == last)
    def _():
        step(True)
        o = acc_ref[...] / l_ref[...]
        for j in range(MLA_HEADS // 2):
            a = _bf(o[:, 2 * j * tq:(2 * j + 1) * tq].T)
            b = _bf(o[:, (2 * j + 1) * tq:(2 * j + 2) * tq].T)
            o_ref[:, j * 2 * V_HEAD:(j + 1) * 2 * V_HEAD] = _dot(jnp.concatenate([a, b], axis=-1), wuv_ref[j])


def _attention(q, kv, latT, w_uv_bd, *, batch, seq, tq, tk):
    n = kv.shape[0]
    nq, nk = seq // tq, seq // tk
    pairs = [(i, j) for i in range(nq) for j in range(((i + 1) * tq - 1) // tk + 1)]
    qi_tab = jnp.asarray([p[0] for p in pairs], jnp.int32)
    ki_tab = jnp.asarray([p[1] for p in pairs], jnp.int32)
    grid_spec = pltpu.PrefetchScalarGridSpec(
        num_scalar_prefetch=2,
        grid=(batch, len(pairs)),
        in_specs=[
            pl.BlockSpec((MLA_HEADS, tq, MLA_ROW), lambda b, p, qi, ki: (0, b * nq + qi[p], 0)),
            pl.BlockSpec((tk, MLA_ROW), lambda b, p, qi, ki: (b * nk + ki[p], 0)),
            pl.BlockSpec((KV_LORA, tk), lambda b, p, qi, ki: (0, b * nk + ki[p])),
            pl.BlockSpec(w_uv_bd.shape, lambda b, p, qi, ki: (0, 0, 0)),
        ],
        out_specs=pl.BlockSpec((tq, MLA_HEADS * V_HEAD), lambda b, p, qi, ki: (b * nq + qi[p], 0)),
        scratch_shapes=[
            pltpu.VMEM((1, MLA_HEADS * tq), _F32),
            pltpu.VMEM((1, MLA_HEADS * tq), _F32),
            pltpu.VMEM((KV_LORA, MLA_HEADS * tq), _F32),
        ],
    )
    return pl.pallas_call(
        functools.partial(_attn_kernel, tq=tq, tk=tk),
        grid_spec=grid_spec,
        out_shape=jax.ShapeDtypeStruct((n, MLA_HEADS * V_HEAD), _F32),
        compiler_params=_params(("parallel", "arbitrary")),
        name="attn_prompt",
    )(qi_tab, ki_tab, q, kv, latT, w_uv_bd)


def _decode_attn_kernel(pt_ref, q_ref, row_ref, cache_ref, o_ref, buf_ref, sem_ref, m_ref, l_ref, acc_ref,
                        *, pages_per_step):
    b = pl.program_id(0)
    c = pl.program_id(1)
    nb = pl.num_programs(0)
    nc = pl.num_programs(1)
    step = b * nc + c
    slot = step % 2

    page_size = cache_ref.shape[2]

    def page_copy(bb, cc, i, sl):
        page = pt_ref[bb, cc * pages_per_step + i]
        dst = buf_ref.at[sl, :, i * page_size:(i + 1) * page_size]
        return pltpu.make_async_copy(cache_ref.at[page], dst, sem_ref.at[sl])

    def start_all(bb, cc, sl):
        for i in range(pages_per_step):
            page_copy(bb, cc, i, sl).start()

    @pl.when(step == 0)
    def _():
        start_all(b, c, slot)

    @pl.when(step + 1 < nb * nc)
    def _():
        nxt = step + 1
        start_all(nxt // nc, nxt % nc, 1 - slot)

    for i in range(pages_per_step):
        page_copy(b, c, i, slot).wait()

    @pl.when(c == 0)
    def _():
        m_ref[...] = jnp.full(m_ref.shape, -jnp.inf, _F32)
        l_ref[...] = jnp.zeros(l_ref.shape, _F32)
        acc_ref[...] = jnp.zeros(acc_ref.shape, _F32)

    q = q_ref[0]
    kvT = _bf(buf_ref[slot])
    s = _dot(q, kvT)
    m_old = m_ref[...]
    m_new = jnp.maximum(m_old, jnp.max(s, axis=-1, keepdims=True))
    alpha = jnp.exp2(m_old - m_new)
    pr = jnp.exp2(s - m_new)
    l_ref[...] = alpha * l_ref[...] + jnp.sum(pr, axis=-1, keepdims=True)
    acc_ref[...] = alpha * acc_ref[...] + _dot_nt(_bf(pr), kvT[:KV_LORA, :])
    m_ref[...] = m_new

    @pl.when(c == nc - 1)
    def _():
        row = _bf(row_ref[0]).astype(_F32)
        s_new = jnp.sum(q.astype(_F32) * row, axis=-1, keepdims=True)
        m_old = m_ref[...]
        m_new = jnp.maximum(m_old, s_new)
        alpha = jnp.exp2(m_old - m_new)
        p_new = jnp.exp2(s_new - m_new)
        l_fin = alpha * l_ref[...] + p_new
        acc = alpha * acc_ref[...] + _bf(p_new).astype(_F32) * row[:, :KV_LORA]
        o_ref[0] = acc / l_fin


def _decode_attention(q, rows, cacheT, page_table):
    b, n_pages = page_table.shape
    page_size = cacheT.shape[2]
    pps = min(DECODE_PAGES_PER_STEP, n_pages)
    assert n_pages % pps == 0
    grid_spec = pltpu.PrefetchScalarGridSpec(
        num_scalar_prefetch=1,
        grid=(b, n_pages // pps),
        in_specs=[
            pl.BlockSpec((1, MLA_HEADS, MLA_ROW), lambda i, c, pt: (i, 0, 0)),
            pl.BlockSpec((1, 1, MLA_ROW), lambda i, c, pt: (i, 0, 0)),
            pl.BlockSpec(memory_space=pl.ANY),
        ],
        out_specs=pl.BlockSpec((1, MLA_HEADS, KV_LORA), lambda i, c, pt: (i, 0, 0)),
        scratch_shapes=[
            pltpu.VMEM((2, MLA_ROW, pps * page_size), _F32),
            pltpu.SemaphoreType.DMA((2,)),
            pltpu.VMEM((MLA_HEADS, 1), _F32),
            pltpu.VMEM((MLA_HEADS, 1), _F32),
            pltpu.VMEM((MLA_HEADS, KV_LORA), _F32),
        ],
    )
    return pl.pallas_call(
        functools.partial(_decode_attn_kernel, pages_per_step=pps),
        grid_spec=grid_spec,
        out_shape=jax.ShapeDtypeStruct((b, MLA_HEADS, KV_LORA), _F32),
        compiler_params=_params(("arbitrary", "arbitrary")),
        name="attn_decode",
    )(page_table, q, rows, cacheT)


def _uv_kernel(o_ref, wuv_ref, out_ref):
    for j in range(MLA_HEADS // 2):
        pair = _bf(o_ref[:, 2 * j * KV_LORA:(2 * j + 2) * KV_LORA])
        out_ref[:, j * 2 * V_HEAD:(j + 1) * 2 * V_HEAD] = _dot(pair, wuv_ref[j])


def _uv(o_lat, w_uv_bd):
    return pl.pallas_call(
        _uv_kernel,
        out_shape=jax.ShapeDtypeStruct((o_lat.shape[0], MLA_HEADS * V_HEAD), _F32),
        name="attn_uv",
    )(o_lat, w_uv_bd)


def _ple_tail(x, p_ref, gp_ref, wgate_ref, wproj_ref):
    gate = _sigmoid(_dot(_bf(_rms(x, gp_ref[...])), wgate_ref[...]))
    return x + gate * _dot(_bf(p_ref[...]), wproj_ref[...])


def _l0_tail_kernel(x_ref, py_ref, ao_ref, p_ref, wout_ref, gf_ref, wg_ref, wu_ref, wd_ref,
                    gp_ref, wgate_ref, wproj_ref, o_ref, x1_ref, h_ref, acc_ref):
    j = pl.program_id(1)
    half = py_ref.shape[1]

    @pl.when(j == 0)
    def _():
        mix = _dot(_bf(py_ref[...]), wout_ref[:half, :]) + _dot(_bf(ao_ref[...]), wout_ref[half:, :])
        x1 = x_ref[...] + mix
        x1_ref[...] = x1
        h_ref[...] = _bf(_rms(x1, gf_ref[...]))
        acc_ref[...] = jnp.zeros(acc_ref.shape, _F32)

    h = h_ref[...]
    act = _silu(_dot(h, wg_ref[...])) * _dot(h, wu_ref[...])
    acc_ref[...] += _dot(_bf(act), wd_ref[...])

    @pl.when(j == pl.num_programs(1) - 1)
    def _():
        o_ref[...] = _ple_tail(x1_ref[...] + acc_ref[...], p_ref, gp_ref, wgate_ref, wproj_ref)


def _l0_tail(x, pool_y, attn_o, p, w_out, g_ffn, w_g, w_u, w_d, g_ple, w_gate, w_proj, *, tm, tf):
    n, d = x.shape
    d_ff = w_g.shape[1]
    row = lambda i, j: (i, 0)
    return pl.pallas_call(
        _l0_tail_kernel,
        grid=(n // tm, d_ff // tf),
        in_specs=[
            pl.BlockSpec((tm, d), row), pl.BlockSpec((tm, pool_y.shape[1]), row),
            pl.BlockSpec((tm, attn_o.shape[1]), row), pl.BlockSpec((tm, p.shape[1]), row),
            _const_spec(w_out.shape), _const_spec(g_ffn.shape),
            pl.BlockSpec((d, tf), lambda i, j: (0, j)), pl.BlockSpec((d, tf), lambda i, j: (0, j)),
            pl.BlockSpec((tf, d), lambda i, j: (j, 0)),
            _const_spec(g_ple.shape), _const_spec(w_gate.shape), _const_spec(w_proj.shape),
        ],
        out_specs=pl.BlockSpec((tm, d), row),
        out_shape=jax.ShapeDtypeStruct((n, d), _F32),
        scratch_shapes=[pltpu.VMEM((tm, d), _F32), pltpu.VMEM((tm, d), _BF16), pltpu.VMEM((tm, d), _F32)],
        compiler_params=_params(("parallel", "arbitrary")),
        name="l0_tail",
    )(x, pool_y, attn_o, p, w_out, g_ffn, w_g, w_u, w_d, g_ple, w_gate, w_proj)


def _route_top2(logits, n_experts):
    lane = lax.broadcasted_iota(jnp.int32, logits.shape, 1)
    lg = jnp.where(lane < n_experts, logits, -jnp.inf)
    m1 = jnp.max(lg, axis=-1, keepdims=True)
    i1 = jnp.min(jnp.where(lg == m1, lane, LANES), axis=-1, keepdims=True)
    lg2 = jnp.where(lane == i1, -jnp.inf, lg)
    m2 = jnp.max(lg2, axis=-1, keepdims=True)
    i2 = jnp.min(jnp.where(lg2 == m2, lane, LANES), axis=-1, keepdims=True)
    e2 = jnp.exp(m2 - m1)
    den = 1.0 + e2
    return jnp.where(lane == i1, 1.0 / den, 0.0) + jnp.where(lane == i2, e2 / den, 0.0)


def _l1_tail_kernel(x_ref, y_ref, p_ref, wout_ref, gf_ref, wr_ref, wg_ref, wu_ref, wd_ref,
                    gp_ref, wgate_ref, wproj_ref, gfin_ref, o_ref, x1_ref, h_ref, comb_ref, acc_ref,
                    *, n_experts):
    e = pl.program_id(1)

    @pl.when(e == 0)
    def _():
        x1 = x_ref[...] + _dot(_bf(y_ref[...]), wout_ref[...])
        x1_ref[...] = x1
        h = _bf(_rms(x1, gf_ref[...]))
        h_ref[...] = h
        comb_ref[...] = _route_top2(_dot(h, wr_ref[...]), n_experts)
        acc_ref[...] = jnp.zeros(acc_ref.shape, _F32)

    h = h_ref[...]
    act = _silu(_dot(h, wg_ref[0])) * _dot(h, wu_ref[0])
    lane = lax.broadcasted_iota(jnp.int32, comb_ref.shape, 1)
    c_e = jnp.sum(jnp.where(lane == e, comb_ref[...], 0.0), axis=-1, keepdims=True)
    acc_ref[...] += c_e * _dot(_bf(act), wd_ref[0])

    @pl.when(e == n_experts - 1)
    def _():
        x3 = _ple_tail(x1_ref[...] + acc_ref[...], p_ref, gp_ref, wgate_ref, wproj_ref)
        o_ref[...] = _rms(x3, gfin_ref[...])


def _l1_tail(x, y, p, w_out, g_ffn, w_router, w_g, w_u, w_d, g_ple, w_gate, w_proj, g_final, *, tm, n_experts):
    n, d = x.shape
    d_e = w_g.shape[2]
    row = lambda i, e: (i, 0)
    return pl.pallas_call(
        functools.partial(_l1_tail_kernel, n_experts=n_experts),
        grid=(n // tm, n_experts),
        in_specs=[
            pl.BlockSpec((tm, d), row), pl.BlockSpec((tm, y.shape[1]), row), pl.BlockSpec((tm, p.shape[1]), row),
            _const_spec(w_out.shape), _const_spec(g_ffn.shape), _const_spec(w_router.shape),
            pl.BlockSpec((1, d, d_e), lambda i, e: (e, 0, 0)), pl.BlockSpec((1, d, d_e), lambda i, e: (e, 0, 0)),
            pl.BlockSpec((1, d_e, d), lambda i, e: (e, 0, 0)),
            _const_spec(g_ple.shape), _const_spec(w_gate.shape), _const_spec(w_proj.shape),
            _const_spec(g_final.shape),
        ],
        out_specs=pl.BlockSpec((tm, d), row),
        out_shape=jax.ShapeDtypeStruct((n, d), _F32),
        scratch_shapes=[pltpu.VMEM((tm, d), _F32), pltpu.VMEM((tm, d), _BF16),
                        pltpu.VMEM((tm, LANES), _F32), pltpu.VMEM((tm, d), _F32)],
        compiler_params=_params(("parallel", "arbitrary")),
        name="l1_tail",
    )(x, y, p, w_out, g_ffn, w_router, w_g, w_u, w_d, g_ple, w_gate, w_proj, g_final)


def _l1_in_kernel(x_ref, g_ref, w_ref, xb_ref, gg_ref):
    d_rnn = xb_ref.shape[1]
    proj = _dot(_bf(_rms(x_ref[...], g_ref[...])), w_ref[...])
    xb_ref[...] = proj[:, :d_rnn]
    gg_ref[...] = _gelu_tanh(proj[:, d_rnn:])


def _l1_in(x, g, w, *, tm):
    n, d = x.shape
    d_rnn = w.shape[1] // 2
    row = lambda i: (i, 0)
    return pl.pallas_call(
        _l1_in_kernel,
        grid=(n // tm,),
        in_specs=[pl.BlockSpec((tm, d), row), _const_spec(g.shape), _const_spec(w.shape)],
        out_specs=[pl.BlockSpec((tm, d_rnn), row), pl.BlockSpec((tm, d_rnn), row)],
        out_shape=[jax.ShapeDtypeStruct((n, d_rnn), _F32), jax.ShapeDtypeStruct((n, d_rnn), _F32)],
        compiler_params=_params(("parallel",)),
        name="l1_in",
    )(x, g, w)


def _lru_gates(xc, wrg_ref, brg_ref, wig_ref, big_ref, lam_ref):
    xcb = _bf(xc)
    rs, igs = [], []
    for j in range(wrg_ref.shape[0]):
        blk = xcb[:, j * 2 * LANES:(j + 1) * 2 * LANES]
        rs.append(_dot(blk, wrg_ref[j]))
        igs.append(_dot(blk, wig_ref[j]))
    r = _sigmoid(jnp.concatenate(rs, axis=-1) + brg_ref[...])
    ig = _sigmoid(jnp.concatenate(igs, axis=-1) + big_ref[...])
    log_a = (-LRU_C * r) * _softplus(-lam_ref[...])
    a = jnp.exp(log_a)
    th = jnp.tanh(log_a)
    mult = jnp.sqrt((-2.0 * th) / (1.0 - th))
    return a, mult * (ig * xc)


def _rglru_seq_kernel(xb_ref, gg_ref, cbuf_ref, h0_ref, cw_ref, cb_ref, wrg_ref, brg_ref, wig_ref, big_ref,
                      lam_ref, y_ref, hl_ref, ext_ref, a_ref, b_ref, hc_ref):
    t = pl.program_id(1)
    tt, d = xb_ref.shape[1], xb_ref.shape[2]
    nbuf = CONV_WIDTH - 1

    @pl.when(t == 0)
    def _():
        ext_ref[0:CONV_HALO - nbuf, :] = jnp.zeros((CONV_HALO - nbuf, d), _F32)
        ext_ref[CONV_HALO - nbuf:CONV_HALO, :] = cbuf_ref[0]
        hc_ref[...] = h0_ref[0]

    ext_ref[CONV_HALO:CONV_HALO + tt, :] = xb_ref[0]
    xc = cb_ref[...] + ext_ref[CONV_HALO - nbuf:CONV_HALO - nbuf + tt, :] * cw_ref[0:1, :]
    for k in range(1, CONV_WIDTH):
        xc = xc + ext_ref[CONV_HALO - nbuf + k:CONV_HALO - nbuf + k + tt, :] * cw_ref[k:k + 1, :]
    a, b = _lru_gates(xc, wrg_ref, brg_ref, wig_ref, big_ref, lam_ref)
    a_ref[...] = a
    b_ref[...] = b
    row = lax.broadcasted_iota(jnp.int32, (SUBLANES, d), 0)

    def group(g, hc):
        r0 = pl.multiple_of(g * SUBLANES, SUBLANES)
        aa = a_ref[pl.ds(r0, SUBLANES), :]
        bb = b_ref[pl.ds(r0, SUBLANES), :]
        for sh in (1, 2, 4):
            keep = row >= sh
            a_s = jnp.where(keep, pltpu.roll(aa, sh, 0), 1.0)
            b_s = jnp.where(keep, pltpu.roll(bb, sh, 0), 0.0)
            bb = aa * b_s + bb
            aa = aa * a_s
        h = aa * hc + bb
        y_ref[0, pl.ds(r0, SUBLANES), :] = h * gg_ref[0, pl.ds(r0, SUBLANES), :]
        return h[SUBLANES - 1:SUBLANES, :]

    hc = lax.fori_loop(0, tt // SUBLANES, group, hc_ref[...])
    hc_ref[...] = hc
    hl_ref[0] = hc
    ext_ref[0:CONV_HALO, :] = ext_ref[tt:tt + CONV_HALO, :]


def _rglru_seq(xb, gg, cbuf, h0, cw, cb, wrg, brg, wig, big, lam, *, tt):
    b, t, d = xb.shape
    seq = lambda i, j: (i, j, 0)
    per_b = lambda i, j: (i, 0, 0)
    return pl.pallas_call(
        _rglru_seq_kernel,
        grid=(b, t // tt),
        in_specs=[
            pl.BlockSpec((1, tt, d), seq), pl.BlockSpec((1, tt, d), seq),
            pl.BlockSpec((1, CONV_WIDTH - 1, d), per_b), pl.BlockSpec((1, 1, d), per_b),
            _const_spec(cw.shape), _const_spec(cb.shape), _const_spec(wrg.shape), _const_spec(brg.shape),
            _const_spec(wig.shape), _const_spec(big.shape), _const_spec(lam.shape),
        ],
        out_specs=[pl.BlockSpec((1, tt, d), seq), pl.BlockSpec((1, 1, d), per_b)],
        out_shape=[jax.ShapeDtypeStruct((b, t, d), _F32), jax.ShapeDtypeStruct((b, 1, d), _F32)],
        scratch_shapes=[pltpu.VMEM((CONV_HALO + tt, d), _F32), pltpu.VMEM((tt, d), _F32),
                        pltpu.VMEM((tt, d), _F32), pltpu.VMEM((1, d), _F32)],
        compiler_params=_params(("parallel", "arbitrary")),
        name="rglru_seq",
    )(xb, gg, cbuf, h0, cw, cb, wrg, brg, wig, big, lam)


def _rglru_step_kernel(xb_ref, gg_ref, cst_ref, h0_ref, cw_ref, cb_ref, wrg_ref, brg_ref, wig_ref, big_ref,
                       lam_ref, y_ref, h_ref):
    xc = cb_ref[...] + xb_ref[...] * cw_ref[CONV_WIDTH - 1:CONV_WIDTH, :]
    for k in range(CONV_WIDTH - 1):
        xc = xc + cst_ref[k] * cw_ref[k:k + 1, :]
    a, b = _lru_gates(xc, wrg_ref, brg_ref, wig_ref, big_ref, lam_ref)
    h = a * h0_ref[...] + b
    h_ref[...] = h
    y_ref[...] = h * gg_ref[...]


def _rglru_step(xb, gg, cst, h0, cw, cb, wrg, brg, wig, big, lam):
    return pl.pallas_call(
        _rglru_step_kernel,
        out_shape=[jax.ShapeDtypeStruct(xb.shape, _F32), jax.ShapeDtypeStruct(xb.shape, _F32)],
        compiler_params=pltpu.CompilerParams(vmem_limit_bytes=VMEM_LIMIT),
        name="rglru_step",
    )(xb, gg, cst, h0, cw, cb, wrg, brg, wig, big, lam)


def _pair_block_diag(blocks):
    n2, r, c = blocks.shape
    b = blocks.reshape(n2 // 2, 2, r, c)
    z = jnp.zeros((n2 // 2, r, c), blocks.dtype)
    top = jnp.concatenate([b[:, 0], z], axis=-1)
    bot = jnp.concatenate([z, b[:, 1]], axis=-1)
    return jnp.concatenate([top, bot], axis=1)


def _rope_tables(pos):
    inv = ROPE_THETA ** (-jnp.arange(0, QK_ROPE, 2, dtype=_F32) / QK_ROPE)
    ang = pos.astype(_F32)[:, None] * inv[None, :]
    cos, sin = jnp.cos(ang), jnp.sin(ang)
    reps = 2 * LANES // QK_ROPE
    return (jnp.tile(jnp.concatenate([cos, cos], axis=-1), (1, reps)),
            jnp.tile(jnp.concatenate([-sin, sin], axis=-1), (1, reps)))


def _prep_weights(prm):
    w = {}
    row2 = lambda v: v.reshape(1, -1)
    w["norm_mix"] = [row2(prm["norm_mix"][i]) for i in range(2)]
    w["norm_ffn"] = [row2(prm["norm_ffn"][i]) for i in range(2)]
    w["norm_ple"] = [row2(prm["norm_ple"][i]) for i in range(2)]
    w["norm_final"] = row2(prm["norm_final"])
    w_in0 = prm["w_in0"][0]
    assert (w_in0.shape[1] - QK_ROPE) % LANES == 0, "rotary key columns must start on a lane-tile boundary"
    w["w_in0"] = _bf(jnp.pad(w_in0, ((0, 0), (0, LANES - QK_ROPE))))
    w["pool_w"] = _bf(prm["pool_w"][0])
    w["pool_scale"] = row2(prm["pool_scale"][0])
    w["q_norm"] = row2(prm["q_norm"][0])
    w["kv_norm"] = row2(prm["kv_norm"][0])
    qh = QK_NOPE + QK_ROPE
    cols = np.concatenate([np.concatenate([h * qh + np.arange(QK_NOPE) for h in range(MLA_HEADS)]),
                           np.concatenate([h * qh + QK_NOPE + np.arange(QK_ROPE) for h in range(MLA_HEADS)])])
    w["w_uq"] = _bf(prm["w_uq"][0][:, cols])
    w["w_uk_bd"] = _bf(_pair_block_diag(jnp.transpose(prm["w_uk"][0], (1, 2, 0))))
    w["w_uv_bd"] = _bf(_pair_block_diag(jnp.transpose(prm["w_uv"][0], (1, 0, 2))))
    w["w_out0"] = _bf(prm["w_out0"][0])
    w["w_ffn_gate"] = _bf(prm["w_ffn_gate"][0])
    w["w_ffn_up"] = _bf(prm["w_ffn_up"][0])
    w["w_ffn_down"] = _bf(prm["w_ffn_down"][0])
    w["w_ple_gate"] = [_bf(prm["w_ple_gate"][i]) for i in range(2)]
    w["w_ple_proj"] = [_bf(prm["w_ple_proj"][i]) for i in range(2)]
    w["w_in1"] = _bf(prm["w_in1"][0])
    w["conv_w"] = prm["conv_w"][0]
    w["conv_b"] = row2(prm["conv_b"][0])
    w["w_rg_bd"] = _bf(_pair_block_diag(prm["w_rg"][0]))
    w["w_ig_bd"] = _bf(_pair_block_diag(prm["w_ig"][0]))
    w["b_rg"] = row2(prm["b_rg"][0])
    w["b_ig"] = row2(prm["b_ig"][0])
    w["lru_lambda"] = row2(prm["lru_lambda"][0])
    w["w_out1"] = _bf(prm["w_out1"][0])
    n_exp = prm["w_router"].shape[2]
    w["w_router"] = _bf(jnp.pad(prm["w_router"][0], ((0, 0), (0, LANES - n_exp))))
    w["n_experts"] = n_exp
    w["w_exp_gate"] = _bf(prm["w_exp_gate"][0])
    w["w_exp_up"] = _bf(prm["w_exp_up"][0])
    w["w_exp_down"] = _bf(prm["w_exp_down"][0])
    return w


def _pick_tile(n, target):
    t = min(n, target)
    while n % t:
        t //= 2
    return t


def _run_group(x, p, pos0, pool_buf, conv_buf, lru_h, cache, page_table, w):
    bsz, t, d = x.shape
    n = bsz * t
    decode = cache is not None
    xf = x.reshape(n, d)
    pf = p.reshape(p.shape[0], n, p.shape[-1])
    tm = _pick_tile(n, 512)
    pool_width = w["pool_scale"].shape[1]

    tab_len = max(t, tm)
    pos = pos0 + (jnp.arange(tab_len, dtype=jnp.int32) % t)
    cos, sin = _rope_tables(pos)
    u, rows, q, kvb, latT = _l0_in(xf, w["norm_mix"][0], w["w_in0"], w["q_norm"], w["w_uq"], w["w_uk_bd"],
                                   w["kv_norm"], cos, sin, tm=tm, pool_width=pool_width)
    if decode:
        assert t == 1
        pool_y = _pool_step(u, jnp.transpose(pool_buf, (1, 0, 2)), w["pool_w"], w["pool_scale"], pos0=pos0)
        o_lat = _decode_attention(jnp.transpose(q, (1, 0, 2)), rows.reshape(n, 1, MLA_ROW),
                                  jnp.swapaxes(cache, 1, 2), page_table)
        attn_o = _uv(o_lat.reshape(n, MLA_HEADS * KV_LORA), w["w_uv_bd"])
        new_pool = jnp.concatenate([pool_buf[:, 1:], u.reshape(bsz, 1, pool_width)], axis=1)
    else:
        u3 = u.reshape(bsz, t, pool_width)
        pool_y = _pool_seq(u3, pool_buf, w["pool_w"], w["pool_scale"], tt=_pick_tile(t, 512), pos0=pos0)
        pool_y = pool_y.reshape(n, pool_width)
        attn_o = _attention(q, kvb, latT, w["w_uv_bd"], batch=bsz, seq=t,
                            tq=_pick_tile(t, 512), tk=_pick_tile(t, 512))
        assert t >= POOL_BUF and t >= CONV_WIDTH - 1
        new_pool = u3[:, t - POOL_BUF:]
    x1 = _l0_tail(xf, pool_y, attn_o, pf[0], w["w_out0"], w["norm_ffn"][0], w["w_ffn_gate"], w["w_ffn_up"],
                  w["w_ffn_down"], w["norm_ple"][0], w["w_ple_gate"][0], w["w_ple_proj"][0],
                  tm=tm, tf=w["w_ffn_gate"].shape[1] // 2)

    xb, gg = _l1_in(x1, w["norm_mix"][1], w["w_in1"], tm=tm)
    d_rnn = xb.shape[1]
    lru_args = (w["conv_w"], w["conv_b"], w["w_rg_bd"], w["b_rg"], w["w_ig_bd"], w["b_ig"], w["lru_lambda"])
    if decode:
        y, h_last = _rglru_step(xb, gg, jnp.transpose(conv_buf, (1, 0, 2)), lru_h, *lru_args)
        new_conv = jnp.concatenate([conv_buf[:, 1:], xb.reshape(bsz, 1, d_rnn)], axis=1)
    else:
        xb3 = xb.reshape(bsz, t, d_rnn)
        y, h_last = _rglru_seq(xb3, gg.reshape(bsz, t, d_rnn), conv_buf, lru_h.reshape(bsz, 1, d_rnn), *lru_args,
                               tt=_pick_tile(t, 256))
        y = y.reshape(n, d_rnn)
        h_last = h_last.reshape(bsz, d_rnn)
        new_conv = xb3[:, t - (CONV_WIDTH - 1):]
    out = _l1_tail(x1, y, pf[1], w["w_out1"], w["norm_ffn"][1], w["w_router"], w["w_exp_gate"], w["w_exp_up"],
                   w["w_exp_down"], w["norm_ple"][1], w["w_ple_gate"][1], w["w_ple_proj"][1], w["norm_final"],
                   tm=tm, n_experts=w["n_experts"])
    return (out.reshape(bsz, t, d), rows.reshape(1, bsz, t, MLA_ROW), new_pool[None], new_conv[None], h_last[None])


def kernel(x_prompt, x_sample, p_prompt, p_sample, cache_mla, state_pool, state_conv, state_lru, page_table, norm_mix, norm_ffn, norm_ple, norm_final, w_in0, pool_w, pool_scale, q_norm, kv_norm, w_uq, w_uk, w_uv, w_out0, w_in1, conv_w, conv_b, w_rg, b_rg, w_ig, b_ig, lru_lambda, w_out1, w_ffn_gate, w_ffn_up, w_ffn_down, w_router, w_exp_gate, w_exp_up, w_exp_down, w_ple_proj, w_ple_gate):
    assert norm_mix.shape[0] == 2, "two trunk layers: pool/MLA then RG-LRU/MoE"
    prm = dict(norm_mix=norm_mix, norm_ffn=norm_ffn, norm_ple=norm_ple, norm_final=norm_final,
               w_in0=w_in0, pool_w=pool_w, pool_scale=pool_scale, q_norm=q_norm, kv_norm=kv_norm,
               w_uq=w_uq, w_uk=w_uk, w_uv=w_uv, w_out0=w_out0, w_in1=w_in1, conv_w=conv_w,
               conv_b=conv_b, w_rg=w_rg, b_rg=b_rg, w_ig=w_ig, b_ig=b_ig, lru_lambda=lru_lambda,
               w_out1=w_out1, w_ffn_gate=w_ffn_gate, w_ffn_up=w_ffn_up, w_ffn_down=w_ffn_down,
               w_router=w_router, w_exp_gate=w_exp_gate, w_exp_up=w_exp_up,
               w_exp_down=w_exp_down, w_ple_proj=w_ple_proj, w_ple_gate=w_ple_gate)
    w = _prep_weights(prm)
    bsz = x_prompt.shape[0]
    dt = x_prompt.dtype
    past_len = page_table.shape[1] * cache_mla.shape[2]
    zero_pool = jnp.zeros((bsz, POOL_BUF, state_pool.shape[-1]), dt)
    zero_conv = jnp.zeros((bsz, CONV_WIDTH - 1, state_conv.shape[-1]), dt)
    zero_h = jnp.zeros((bsz, state_lru.shape[-1]), dt)
    y_p, rows_p, pool_p, conv_p, h_p = _run_group(
        x_prompt, p_prompt, 0, zero_pool, zero_conv, zero_h, None, None, w)
    y_s, rows_s, pool_s, conv_s, h_s = _run_group(
        x_sample, p_sample, past_len, state_pool[0], state_conv[0], state_lru[0],
        cache_mla.reshape(cache_mla.shape[1:]), page_table, w)
    return (y_p, y_s, rows_p, rows_s, pool_p, pool_s, conv_p, conv_s, h_p, h_s)
```
